```python
import jax, jax.numpy as jnp
from jax import lax
import numpy as np

D_MODEL = 4096
BATCH = 1
SEQ = 16384
DEPTH = 4

RW_HEADS = 32
RW_HEAD = 64
RW_DIM = RW_HEADS * RW_HEAD
DECAY_LORA = 96
AAA_LORA = 96
GATE_LORA = 256
GN_EPS = 64e-5
MLA_HEADS = 16
Q_LORA = 1536
KV_LORA = 512
QK_NOPE = 128
QK_ROPE = 64
V_HEAD = 128
ROPE_THETA = 10000.0
ATTN_BLOCK = 128
RW_PROJ = 3 * RW_DIM + DECAY_LORA + AAA_LORA + GATE_LORA
MLA_PROJ = Q_LORA + KV_LORA + QK_ROPE
EVEN_PROJ = RW_PROJ + MLA_PROJ
MIX_OUT = RW_DIM + MLA_HEADS * V_HEAD
CHUNK = 128
SG_DIM = D_MODEL
SG_GROUPS = 32
SG_GROUP_DIM = SG_DIM // SG_GROUPS
LN_EPS = 1e-5
D_FF = 11008
N_EXPERTS = 8
TOP_K = 2
D_FF_EXPERT = 1536
PLE_DIM = 256
N_EVEN = (DEPTH + 1) // 2
N_ODD = DEPTH // 2
RMS_EPS = 1e-6

kernel_name = 'hybrid_rwkv7_mla_gmlp_moe_ple'


def rms_norm(x, g):
    xf = x.astype(jnp.float32)
    y = xf * lax.rsqrt(jnp.mean(xf * xf, axis=-1, keepdims=True) + RMS_EPS)
    return (y * g.astype(jnp.float32)).astype(x.dtype)


def token_shift(y):
    return jnp.pad(y, ((0, 0), (1, 0), (0, 0)))[:, :-1]


def rope_tables(positions):
    inv = 1.0 / (ROPE_THETA ** (jnp.arange(0, QK_ROPE, 2, dtype=jnp.float32) / QK_ROPE))
    ang = positions.astype(jnp.float32)[..., None] * inv
    return jnp.cos(ang), jnp.sin(ang)


def apply_rope(x, cos, sin):
    x1, x2 = jnp.split(x.astype(jnp.float32), 2, axis=-1)
    return jnp.concatenate([x1 * cos - x2 * sin, x2 * cos + x1 * sin], axis=-1).astype(x.dtype)


def rwkv7_recurrence(r, w, k, v, a, b):
    bsz, _, nh, n = r.shape

    def step(state, inp):
        r_t, w_t, k_t, v_t, a_t, b_t = inp
        sa = jnp.einsum('bhij,bhj->bhi', state, a_t)
        state = (state * w_t[:, :, None, :] + sa[..., None] * b_t[:, :, None, :]
                 + v_t[..., None] * k_t[:, :, None, :])
        return state, jnp.einsum('bhij,bhj->bhi', state, r_t)

    xs = tuple(jnp.moveaxis(t, 1, 0) for t in (r, w, k, v, a, b))
    s0 = jnp.zeros((bsz, nh, n, n), jnp.float32)
    _, out = lax.scan(step, s0, xs)
    return jnp.moveaxis(out, 0, 1)


def rwkv7_group(y, mu, w0, w2, a0, a2, g2, k_k, k_a, r_k, lnx_w, lnx_b):
    bsz, s, _ = y.shape
    f32 = jnp.float32
    y = y + (token_shift(y) - y) * mu
    r, k, v, w_lo, a_lo, g_lo = jnp.split(
        y, [RW_DIM, 2 * RW_DIM, 3 * RW_DIM, 3 * RW_DIM + DECAY_LORA,
            3 * RW_DIM + DECAY_LORA + AAA_LORA], axis=-1)
    w_log = -jax.nn.softplus(-(w0 + jnp.tanh(w_lo) @ w2).astype(f32)) - 0.5
    decay = jnp.exp(-jnp.exp(w_log))
    a = jax.nn.sigmoid((a0 + a_lo @ a2).astype(f32))
    g = (jax.nn.sigmoid(g_lo) @ g2).astype(f32)
    heads = lambda t: t.astype(f32).reshape(bsz, s, RW_HEADS, RW_HEAD)
    r, k, v, decay, a = heads(r), heads(k), heads(v), heads(decay), heads(a)
    k_k_h = k_k.astype(f32).reshape(RW_HEADS, RW_HEAD)
    k_a_h = k_a.astype(f32).reshape(RW_HEADS, RW_HEAD)
    kk = k * k_k_h
    kk = kk * lax.rsqrt(jnp.maximum(jnp.sum(kk * kk, axis=-1, keepdims=True), 1e-24))
    k = k * (1.0 + (a - 1.0) * k_a_h)
    o = rwkv7_recurrence(r, decay, k, v, -kk, kk * a)
    mean = jnp.mean(o, axis=-1, keepdims=True)
    var = jnp.mean(jnp.square(o - mean), axis=-1, keepdims=True)
    o = ((o - mean) * lax.rsqrt(var + GN_EPS)).reshape(bsz, s, RW_DIM)
    o = o * lnx_w.astype(f32) + lnx_b.astype(f32)
    bonus = jnp.sum(r * k * r_k.astype(f32), axis=-1, keepdims=True) * v
    o = (o + bonus.reshape(bsz, s, RW_DIM)) * g
    return o.astype(y.dtype)


def mla_group(y, cos, sin, q_norm, w_uq, kv_norm, w_ukv):
    bsz, s, _ = y.shape
    cq, ckv, k_pe = jnp.split(y, [Q_LORA, Q_LORA + KV_LORA], axis=-1)
    q = (rms_norm(cq, q_norm) @ w_uq).reshape(bsz, s, MLA_HEADS, QK_NOPE + QK_ROPE)
    q_nope, q_pe = jnp.split(q, [QK_NOPE], axis=-1)
    q_pe = apply_rope(q_pe, cos[:, :, None, :], sin[:, :, None, :])
    kv = (rms_norm(ckv, kv_norm) @ w_ukv).reshape(bsz, s, MLA_HEADS, QK_NOPE + V_HEAD)
    k_nope, v = jnp.split(kv, [QK_NOPE], axis=-1)
    k_pe = apply_rope(k_pe, cos, sin)
    scale = (QK_NOPE + QK_ROPE) ** -0.5
    q = jnp.concatenate([q_nope, q_pe], axis=-1) * scale
    k = jnp.concatenate(
        [k_nope, jnp.broadcast_to(k_pe[:, :, None, :], (bsz, s, MLA_HEADS, QK_ROPE))], axis=-1)
    nb = s // ATTN_BLOCK
    q_blocks = jnp.moveaxis(q.reshape(bsz, nb, ATTN_BLOCK, MLA_HEADS, QK_NOPE + QK_ROPE), 1, 0)
    key_pos = jnp.arange(s)
    neg = jnp.finfo(jnp.float32).min

    def attend(args):
        qb, bi = args
        sc = jnp.einsum('bqhd,bkhd->bhqk', qb, k, preferred_element_type=jnp.float32)
        q_pos = bi * ATTN_BLOCK + jnp.arange(ATTN_BLOCK)
        sc = jnp.where(key_pos[None, :] <= q_pos[:, None], sc, neg)
        pr = jax.nn.softmax(sc, axis=-1)
        return jnp.einsum('bhqk,bkhd->bqhd', pr.astype(v.dtype), v)

    o = lax.map(attend, (q_blocks, jnp.arange(nb)))
    return jnp.moveaxis(o, 0, 1).reshape(bsz, s, MLA_HEADS * V_HEAD)


def spatial_gating(hn, w_in, ln_w, ln_b, w_sp, b_sp, w_out):
    bsz, s, _ = hn.shape
    z = jax.nn.gelu(hn @ w_in, approximate=False)
    u, v = jnp.split(z, 2, axis=-1)
    vf = v.astype(jnp.float32)
    mean = jnp.mean(vf, axis=-1, keepdims=True)
    var = jnp.mean(jnp.square(vf - mean), axis=-1, keepdims=True)
    v = ((vf - mean) * lax.rsqrt(var + LN_EPS) * ln_w + ln_b).astype(z.dtype)
    v = v.reshape(bsz, s // CHUNK, CHUNK, SG_GROUPS, SG_GROUP_DIM)
    causal = jnp.tril(jnp.ones((CHUNK, CHUNK), dtype=bool))
    w = jnp.where(causal[None], w_sp, jnp.zeros_like(w_sp))
    sg = jnp.einsum('gts,bcsgd->bctgd', w, v) + b_sp.T[:, :, None]
    return (u * sg.reshape(bsz, s, SG_DIM)) @ w_out


def swiglu(x, w_gate, w_up, w_down):
    return (jax.nn.silu(x @ w_gate) * (x @ w_up)) @ w_down


def moe_swiglu(x, w_router, w_gate, w_up, w_down):
    logits = (x @ w_router).astype(jnp.float32)
    top_val, top_idx = lax.top_k(logits, TOP_K)
    top_w = jax.nn.softmax(top_val, axis=-1)
    gates = jnp.sum(jax.nn.one_hot(top_idx, N_EXPERTS, dtype=jnp.float32) * top_w[..., None], axis=-2)
    y = jnp.zeros_like(x)
    for e in range(N_EXPERTS):
        y = y + gates[..., e:e + 1].astype(x.dtype) * swiglu(x, w_gate[e], w_up[e], w_down[e])
    return y


def setup_inputs(seed: int = 0) -> dict:
    key = jax.random.key(seed)
    keys = jax.random.split(key, 48)
    cnt = [0]
    f32 = jnp.float32

    def nk():
        cnt[0] += 1
        return keys[cnt[0] - 1]

    def nrm(shape, fan_in, scale=1.0):
        return jax.random.normal(nk(), shape, f32) * (scale * fan_in ** -0.5)

    def gain(shape):
        return 1.0 + 0.05 * jax.random.normal(nk(), shape, f32)

    def small(shape, s=0.02):
        return s * jax.random.normal(nk(), shape, f32)

    def unif(shape, lo, hi):
        return jax.random.uniform(nk(), shape, f32, lo, hi)

    x = jax.random.normal(nk(), (BATCH, SEQ, D_MODEL), f32)
    p = jax.random.normal(nk(), (DEPTH, BATCH, SEQ, PLE_DIM), f32)
    offset = jax.random.randint(nk(), (BATCH, 1), 0, 4096, dtype=jnp.int32)
    positions = offset + jnp.arange(SEQ, dtype=jnp.int32)[None, :]
    return {
        'x': x,
        'p': p,
        'positions': positions,
        'norm_mix': gain((DEPTH, D_MODEL)),
        'norm_ffn': gain((DEPTH, D_MODEL)),
        'norm_ple': gain((DEPTH, D_MODEL)),
        'final_norm': gain((D_MODEL,)),
        'ev_in_proj': nrm((N_EVEN, D_MODEL, EVEN_PROJ), D_MODEL),
        'rw_mu': unif((N_EVEN, RW_PROJ), 0.0, 1.0),
        'rw_w0': unif((N_EVEN, RW_DIM), -6.0, -1.0),
        'rw_w2': nrm((N_EVEN, DECAY_LORA, RW_DIM), DECAY_LORA, 0.5),
        'rw_a0': small((N_EVEN, RW_DIM), 0.1),
        'rw_a2': nrm((N_EVEN, AAA_LORA, RW_DIM), AAA_LORA, 0.5),
        'rw_g2': nrm((N_EVEN, GATE_LORA, RW_DIM), GATE_LORA),
        'rw_k_k': 0.85 + small((N_EVEN, RW_DIM), 0.05),
        'rw_k_a': gain((N_EVEN, RW_DIM)),
        'rw_r_k': small((N_EVEN, RW_HEADS, RW_HEAD), 0.1),
        'rw_lnx_w': gain((N_EVEN, RW_DIM)),
        'rw_lnx_b': small((N_EVEN, RW_DIM)),
        'mla_q_norm': gain((N_EVEN, Q_LORA)),
        'mla_w_uq': nrm((N_EVEN, Q_LORA, MLA_HEADS * (QK_NOPE + QK_ROPE)), Q_LORA),
        'mla_kv_norm': gain((N_EVEN, KV_LORA)),
        'mla_w_ukv': nrm((N_EVEN, KV_LORA, MLA_HEADS * (QK_NOPE + V_HEAD)), KV_LORA),
        'ev_out_proj': nrm((N_EVEN, MIX_OUT, D_MODEL), MIX_OUT),
        'ffn_w_gate': nrm((N_EVEN, D_MODEL, D_FF), D_MODEL),
        'ffn_w_up': nrm((N_EVEN, D_MODEL, D_FF), D_MODEL),
        'ffn_w_down': nrm((N_EVEN, D_FF, D_MODEL), D_FF),
        'sg_in_proj': nrm((N_ODD, D_MODEL, 2 * SG_DIM), D_MODEL),
        'sg_ln_w': gain((N_ODD, SG_DIM)),
        'sg_ln_b': small((N_ODD, SG_DIM)),
        'sg_w_spatial': nrm((N_ODD, SG_GROUPS, CHUNK, CHUNK), CHUNK),
        'sg_b_spatial': gain((N_ODD, SG_GROUPS, CHUNK)),
        'sg_out_proj': nrm((N_ODD, SG_DIM, D_MODEL), SG_DIM),
        'moe_router': nrm((N_ODD, D_MODEL, N_EXPERTS), D_MODEL),
        'moe_w_gate': nrm((N_ODD, N_EXPERTS, D_MODEL, D_FF_EXPERT), D_MODEL),
        'moe_w_up': nrm((N_ODD, N_EXPERTS, D_MODEL, D_FF_EXPERT), D_MODEL),
        'moe_w_down': nrm((N_ODD, N_EXPERTS, D_FF_EXPERT, D_MODEL), D_FF_EXPERT),
        'ple_proj': nrm((DEPTH, PLE_DIM, D_MODEL), PLE_DIM, 0.5),
        'ple_gate': nrm((DEPTH, D_MODEL, D_MODEL), D_MODEL),
    }


def reference(x, p, positions, norm_mix, norm_ffn, norm_ple, final_norm,
              ev_in_proj, rw_mu, rw_w0, rw_w2, rw_a0, rw_a2, rw_g2, rw_k_k, rw_k_a,
              rw_r_k, rw_lnx_w, rw_lnx_b, mla_q_norm, mla_w_uq, mla_kv_norm, mla_w_ukv,
              ev_out_proj, ffn_w_gate, ffn_w_up, ffn_w_down,
              sg_in_proj, sg_ln_w, sg_ln_b, sg_w_spatial, sg_b_spatial, sg_out_proj,
              moe_router, moe_w_gate, moe_w_up, moe_w_down, ple_proj, ple_gate):
    cos, sin = rope_tables(positions)
    h = x
    for i in range(DEPTH):
        j = i // 2
        hn = rms_norm(h, norm_mix[i])
        if i % 2 == 0:
            y = hn @ ev_in_proj[j]
            y_rw, y_mla = jnp.split(y, [RW_PROJ], axis=-1)
            o_rw = rwkv7_group(y_rw, rw_mu[j], rw_w0[j], rw_w2[j], rw_a0[j], rw_a2[j], rw_g2[j],
                               rw_k_k[j], rw_k_a[j], rw_r_k[j], rw_lnx_w[j], rw_lnx_b[j])
            o_mla = mla_group(y_mla, cos, sin, mla_q_norm[j], mla_w_uq[j], mla_kv_norm[j], mla_w_ukv[j])
            h = h + jnp.concatenate([o_rw, o_mla.astype(o_rw.dtype)], axis=-1) @ ev_out_proj[j]
            h = h + swiglu(rms_norm(h, norm_ffn[i]), ffn_w_gate[j], ffn_w_up[j], ffn_w_down[j])
        else:
            h = h + spatial_gating(hn, sg_in_proj[j], sg_ln_w[j], sg_ln_b[j],
                                   sg_w_spatial[j], sg_b_spatial[j], sg_out_proj[j])
            h = h + moe_swiglu(rms_norm(h, norm_ffn[i]), moe_router[j],
                               moe_w_gate[j], moe_w_up[j], moe_w_down[j])
        gate = jax.nn.sigmoid(rms_norm(h, norm_ple[i]) @ ple_gate[i])
        h = h + gate * (p[i] @ ple_proj[i])
    return rms_norm(h, final_norm)
```

```python
import functools

import jax
import jax.numpy as jnp
from jax import lax
from jax.experimental import pallas as pl
from jax.experimental.pallas import tpu as pltpu

D_MODEL = 4096
DEPTH = 4
RW_HEADS = 32
RW_HEAD = 64
RW_DIM = RW_HEADS * RW_HEAD
DECAY_LORA = 96
AAA_LORA = 96
GATE_LORA = 256
GN_EPS = 64e-5
MLA_HEADS = 16
Q_LORA = 1536
KV_LORA = 512
QK_NOPE = 128
QK_ROPE = 64
V_HEAD = 128
ROPE_THETA = 10000.0
CHUNK = 128
SG_GROUPS = 32
LN_EPS = 1e-5
D_FF = 11008
N_EXPERTS = 8
TOP_K = 2
D_FF_EXPERT = 1536
PLE_DIM = 256
RMS_EPS = 1e-6

V7X_VMEM_BYTES = 64 * 1024 * 1024
LANES = 128
RW_CHUNK = 64
FLASH_TQ = 512
FLASH_TK = 512

F32 = jnp.float32
BF16 = jnp.bfloat16


def _pick(dim, candidates):
    for c in candidates:
        if dim % c == 0:
            return c
    return dim


def _params(sem, vmem_bytes):
    limit = int(min(V7X_VMEM_BYTES - 6 * 1024 * 1024, max(vmem_bytes * 5 // 4 + (2 << 20), 16 << 20)))
    return pltpu.CompilerParams(dimension_semantics=sem, vmem_limit_bytes=limit)


def _nbytes(shape, dtype):
    n = 1
    for s in shape:
        n *= s
    return n * jnp.dtype(dtype).itemsize


def _rmsnorm_kernel(x_ref, g_ref, o_ref, *, eps):
    x = x_ref[...].astype(F32)
    ms = jnp.mean(x * x, axis=-1, keepdims=True)
    o_ref[...] = (x * lax.rsqrt(ms + eps) * g_ref[...]).astype(o_ref.dtype)


def rmsnorm(x, g, out_dtype, *, width=None, col_block=0):
    m = x.shape[0]
    d = x.shape[1] if width is None else width
    tm = _pick(m, (512, 256, 128, 64, 32, 16, 8))
    vm = 2 * (_nbytes((tm, d), x.dtype) + _nbytes((tm, d), out_dtype)) + 4 * _nbytes((tm, d), F32)
    return pl.pallas_call(
        functools.partial(_rmsnorm_kernel, eps=RMS_EPS),
        out_shape=jax.ShapeDtypeStruct((m, d), out_dtype),
        grid=(m // tm,),
        in_specs=[pl.BlockSpec((tm, d), lambda i: (i, col_block)),
                  pl.BlockSpec((1, d), lambda i: (0, 0))],
        out_specs=pl.BlockSpec((tm, d), lambda i: (i, 0)),
        compiler_params=_params(("parallel",), vm),
        name="rmsnorm",
    )(x, g.reshape(1, d).astype(F32))


def _mm_kernel(*refs, nk, n_w, n_extra, epi):
    a_ref = refs[0]
    w_refs = refs[1:1 + n_w]
    extra_refs = refs[1 + n_w:1 + n_w + n_extra]
    o_ref = refs[1 + n_w + n_extra]
    acc_refs = refs[2 + n_w + n_extra:]
    a = a_ref[...]
    if nk == 1:
        accs = [jnp.dot(a, w[...], preferred_element_type=F32) for w in w_refs]
        o_ref[...] = epi(accs, [e[...] for e in extra_refs]).astype(o_ref.dtype)
        return
    k = pl.program_id(2)

    @pl.when(k == 0)
    def _():
        for acc in acc_refs:
            acc[...] = jnp.zeros_like(acc)

    for acc, w in zip(acc_refs, w_refs):
        acc[...] += jnp.dot(a, w[...], preferred_element_type=F32)

    @pl.when(k == nk - 1)
    def _():
        o_ref[...] = epi([acc[...] for acc in acc_refs], [e[...] for e in extra_refs]).astype(o_ref.dtype)


def matmul(a, ws, epi, out_dtype, *, extras=(), tm=1024, tn=512, tk=None, a_col_block=0, a_width=None,
           w_specs=None, n_out=None, name="matmul"):
    m = a.shape[0]
    kdim = a.shape[1] if a_width is None else a_width
    n = ws[0].shape[-1] if n_out is None else n_out
    tm = _pick(m, (tm, 512, 256, 128, 64, 32, 16, 8))
    tn = _pick(n, (tn, 512, 256, 128))
    tk = kdim if tk is None else _pick(kdim, (tk,))
    nk = kdim // tk
    kb0 = a_col_block * nk
    if nk == 1:
        grid = (m // tm, n // tn)
        a_spec = pl.BlockSpec((tm, tk), lambda i, j: (i, kb0))
        w_spec_list = w_specs or [pl.BlockSpec((tk, tn), lambda i, j: (0, j)) for _ in ws]
        o_spec = pl.BlockSpec((tm, tn), lambda i, j: (i, j))
        sem = ("parallel", "parallel")
        scratch = []
    else:
        grid = (m // tm, n // tn, nk)
        a_spec = pl.BlockSpec((tm, tk), lambda i, j, k: (i, kb0 + k))
        w_spec_list = w_specs or [pl.BlockSpec((tk, tn), lambda i, j, k: (k, j)) for _ in ws]
        o_spec = pl.BlockSpec((tm, tn), lambda i, j, k: (i, j))
        sem = ("parallel", "parallel", "arbitrary")
        scratch = [pltpu.VMEM((tm, tn), F32) for _ in ws]
    extra_specs = [pl.BlockSpec(bs, im) for (_, bs, im) in extras]
    vm = 2 * _nbytes((tm, tk), a.dtype) + 2 * len(ws) * _nbytes((tk, tn), ws[0].dtype)
    vm += 2 * _nbytes((tm, tn), out_dtype) + (len(ws) + 2) * _nbytes((tm, tn), F32)
    for (arr, bs, _) in extras:
        vm += 2 * _nbytes([1 if b is None else b for b in bs], arr.dtype)
    return pl.pallas_call(
        functools.partial(_mm_kernel, nk=nk, n_w=len(ws), n_extra=len(extras), epi=epi),
        out_shape=jax.ShapeDtypeStruct((m, n), out_dtype),
        grid=grid,
        in_specs=[a_spec] + list(w_spec_list) + extra_specs,
        out_specs=o_spec,
        scratch_shapes=scratch,
        compiler_params=_params(sem, vm),
        name=name,
    )(a, *ws, *[e[0] for e in extras])


def _epi_id(accs, extras):
    return accs[0]


def _epi_residual(accs, extras):
    return extras[0] + accs[0]


def _epi_swiglu(accs, extras):
    g, u = accs
    return g * jax.nn.sigmoid(g) * u


def _epi_gelu(accs, extras):
    z = accs[0]
    return 0.5 * z * (1.0 + lax.erf(z * (2.0 ** -0.5)))


def _res_extra(h, tm, tn, nk):
    if nk == 1:
        return (h, (tm, tn), lambda i, j: (i, j))
    return (h, (tm, tn), lambda i, j, k: (i, j))


def _ple_kernel(hn_ref, wg_ref, p_ref, wp_ref, h_ref, o_ref):
    gate = jax.nn.sigmoid(jnp.dot(hn_ref[...], wg_ref[...], preferred_element_type=F32))
    proj = jnp.dot(p_ref[...], wp_ref[...], preferred_element_type=F32)
    o_ref[...] = h_ref[...] + gate * proj


def ple(hn, w_gate, p, w_proj, h):
    m, d = hn.shape
    n = w_gate.shape[1]
    pd = p.shape[1]
    tm = _pick(m, (1024, 512, 256, 128, 64, 32, 16, 8))
    tn = _pick(n, (512, 256, 128))
    vm = 2 * (_nbytes((tm, d), BF16) + _nbytes((d, tn), BF16) + _nbytes((tm, pd), BF16) + _nbytes((pd, tn), BF16))
    vm += 6 * _nbytes((tm, tn), F32)
    return pl.pallas_call(
        _ple_kernel,
        out_shape=jax.ShapeDtypeStruct((m, n), F32),
        grid=(m // tm, n // tn),
        in_specs=[pl.BlockSpec((tm, d), lambda i, j: (i, 0)),
                  pl.BlockSpec((d, tn), lambda i, j: (0, j)),
                  pl.BlockSpec((tm, pd), lambda i, j: (i, 0)),
                  pl.BlockSpec((pd, tn), lambda i, j: (0, j)),
                  pl.BlockSpec((tm, tn), lambda i, j: (i, j))],
        out_specs=pl.BlockSpec((tm, tn), lambda i, j: (i, j)),
        compiler_params=_params(("parallel", "parallel"), vm),
        name="ple",
    )(hn, w_gate, p, w_proj, h)


def _bdot(a, b):
    return jnp.dot(a.astype(BF16), b.astype(BF16), preferred_element_type=F32)


def _bdot_nt(a, b):
    return lax.dot_general(a.astype(BF16), b.astype(BF16), (((1,), (1,)), ((), ())),
                           preferred_element_type=F32)


def _bdot_tn(a, b):
    return lax.dot_general(a.astype(BF16), b.astype(BF16), (((0,), (0,)), ((), ())),
                           preferred_element_type=F32)


def _split3(x):
    hi = x.astype(BF16)
    r1 = x - hi.astype(F32)
    mid = r1.astype(BF16)
    lo = (r1 - mid.astype(F32)).astype(BF16)
    return hi, mid, lo


def _rwkv_a_kernel(r_ref, k_ref, v_ref, a_ref, b_ref, lw_ref, mt_ref, nt_ref, q_ref, o0_ref, *, hb):
    t = RW_CHUNK
    n = RW_HEAD
    row = lax.broadcasted_iota(jnp.int32, (t, t), 0)
    col = lax.broadcasted_iota(jnp.int32, (t, t), 1)
    incl = col <= row
    strict = col < row
    tri = incl.astype(BF16)
    eye = (row == col).astype(F32)
    drow = lax.broadcasted_iota(jnp.int32, (n, n), 0)
    dcol = lax.broadcasted_iota(jnp.int32, (n, n), 1)
    for h in range(hb):
        lw = lw_ref[h]
        hi, mid, lo = _split3(lw)
        cum = (jnp.dot(tri, hi, preferred_element_type=F32) + jnp.dot(tri, mid, preferred_element_type=F32)
               + jnp.dot(tri, lo, preferred_element_type=F32))
        cum_t = cum[t - 1:t, :]
        e_in = jnp.exp(cum)
        e_out = jnp.exp(-cum)
        e_prev = jnp.exp(cum - lw)
        e_rem = jnp.exp(cum_t - cum)
        p_t = jnp.exp(cum_t)
        r = r_ref[h]
        k = k_ref[h]
        v = v_ref[h]
        a = a_ref[h]
        b = b_ref[h]
        rh = r * e_in
        ah = a * e_prev
        bh = b * e_out
        kh = k * e_out
        bt = b * e_rem
        kt = k * e_rem
        n_ab = jnp.where(strict, _bdot_nt(ah, bh), 0.0)
        n_ak = jnp.where(strict, _bdot_nt(ah, kh), 0.0)
        l_rb = jnp.where(incl, _bdot_nt(rh, bh), 0.0)
        l_rk = jnp.where(incl, _bdot_nt(rh, kh), 0.0)
        x = eye + n_ab
        pw = n_ab
        steps = (t - 1).bit_length() - 1
        for _ in range(steps):
            pw = _bdot(pw, pw)
            x = x + _bdot(x, pw)
        akv = _bdot(n_ak, v)
        w = _bdot(x, ah)
        u0 = _bdot(x, akv)
        q_ref[h] = rh + _bdot(l_rb, w)
        o0_ref[h] = _bdot(l_rb, u0) + _bdot(l_rk, v)
        mt_ref[h, 0] = jnp.where(drow == dcol, p_t, 0.0) + _bdot_tn(bt, w)
        nt_ref[h, 0] = _bdot_tn(bt, u0) + _bdot_tn(kt, v)


def _rwkv_b_kernel(mt_ref, nt_ref, q_ref, o0_ref, o_ref, z_ref, *, nh):
    c = pl.program_id(0)

    @pl.when(c == 0)
    def _():
        z_ref[...] = jnp.zeros_like(z_ref)

    for h in range(nh):
        z = z_ref[h]
        z_hi = z.astype(BF16)
        z_lo = (z - z_hi.astype(F32)).astype(BF16)
        q = q_ref[h]
        q_hi = q.astype(BF16)
        q_lo = (q - q_hi.astype(F32)).astype(BF16)
        o_ref[h] = (o0_ref[h] + jnp.dot(q_hi, z_hi, preferred_element_type=F32)
                    + jnp.dot(q_hi, z_lo, preferred_element_type=F32)
                    + jnp.dot(q_lo, z_hi, preferred_element_type=F32))
        m = mt_ref[h, 0]
        m_hi = m.astype(BF16)
        m_lo = (m - m_hi.astype(F32)).astype(BF16)
        z_ref[h] = (nt_ref[h, 0] + jnp.dot(m_hi, z_hi, preferred_element_type=F32)
                    + jnp.dot(m_hi, z_lo, preferred_element_type=F32)
                    + jnp.dot(m_lo, z_hi, preferred_element_type=F32))


def rwkv_recurrence(r, k, v, a, b, lw):
    nh, s, n = r.shape
    t = RW_CHUNK
    nc = s // t
    hb = _pick(nh, (4, 2, 1))
    seq_spec = pl.BlockSpec((hb, t, n), lambda g, c: (g, c, 0))
    mat_spec = pl.BlockSpec((hb, 1, n, n), lambda g, c: (g, c, 0, 0))
    mt, nt, qt, o0 = pl.pallas_call(
        functools.partial(_rwkv_a_kernel, hb=hb),
        out_shape=(jax.ShapeDtypeStruct((nh, nc, n, n), F32), jax.ShapeDtypeStruct((nh, nc, n, n), F32),
                   jax.ShapeDtypeStruct((nh, s, n), F32), jax.ShapeDtypeStruct((nh, s, n), F32)),
        grid=(nh // hb, nc),
        in_specs=[seq_spec] * 6,
        out_specs=(mat_spec, mat_spec, seq_spec, seq_spec),
        compiler_params=_params(("parallel", "parallel"), 32 << 20),
        name="rwkv_chunk_local",
    )(r, k, v, a, b, lw)
    seq_all = pl.BlockSpec((nh, t, n), lambda c: (0, c, 0))
    mat_all = pl.BlockSpec((nh, 1, n, n), lambda c: (0, c, 0, 0))
    return pl.pallas_call(
        functools.partial(_rwkv_b_kernel, nh=nh),
        out_shape=jax.ShapeDtypeStruct((nh, s, n), F32),
        grid=(nc,),
        in_specs=[mat_all, mat_all, seq_all, seq_all],
        out_specs=seq_all,
        scratch_shapes=[pltpu.VMEM((nh, n, n), F32)],
        compiler_params=_params(("arbitrary",), 32 << 20),
        name="rwkv_state_scan",
    )(mt, nt, qt, o0)


def _flash_kernel(qi_ref, kj_ref, q_ref, k_ref, v_ref, o_ref, m_sc, l_sc, acc_sc, *, tq, tk):
    step = pl.program_id(1)
    i = qi_ref[step]
    j = kj_ref[step]

    @pl.when(j == 0)
    def _():
        m_sc[...] = jnp.full_like(m_sc, -1e30)
        l_sc[...] = jnp.zeros_like(l_sc)
        acc_sc[...] = jnp.zeros_like(acc_sc)

    def update(masked):
        s = lax.dot_general(q_ref[0], k_ref[0], (((1,), (1,)), ((), ())), preferred_element_type=F32)
        if masked:
            rows = i * tq + lax.broadcasted_iota(jnp.int32, (tq, tk), 0)
            cols = j * tk + lax.broadcasted_iota(jnp.int32, (tq, tk), 1)
            s = jnp.where(cols <= rows, s, -1e30)
        m_prev = m_sc[...]
        m_new = jnp.maximum(m_prev, jnp.max(s, axis=1, keepdims=True))
        alpha = jnp.exp(m_prev - m_new)
        p = jnp.exp(s - m_new)
        l_sc[...] = alpha * l_sc[...] + jnp.sum(p, axis=1, keepdims=True)
        acc_sc[...] = alpha * acc_sc[...] + jnp.dot(p.astype(BF16), v_ref[0], preferred_element_type=F32)
        m_sc[...] = m_new

    full = (j + 1) * tk <= i * tq + 1

    @pl.when(full)
    def _():
        update(False)

    @pl.when(jnp.logical_not(full))
    def _():
        update(True)

    last = ((i + 1) * tq - 1) // tk

    @pl.when(j == last)
    def _():
        o_ref[0] = (acc_sc[...] / l_sc[...]).astype(o_ref.dtype)


def flash_attention(q, k, v):
    nh, s, dk = q.shape
    dv = v.shape[-1]
    tq = _pick(s, (FLASH_TQ, 256, 128))
    tk = _pick(tq, (FLASH_TK, 256, 128))
    qi, kj = [], []
    for i in range(s // tq):
        for j in range(((i + 1) * tq - 1) // tk + 1):
            qi.append(i)
            kj.append(j)
    qi = jnp.asarray(qi, jnp.int32)
    kj = jnp.asarray(kj, jnp.int32)
    vm = 2 * (_nbytes((tq, dk), BF16) + _nbytes((tk, dk), BF16) + _nbytes((tk, dv), BF16) + _nbytes((tq, dv), BF16))
    vm += 6 * _nbytes((tq, tk), F32) + 3 * _nbytes((tq, LANES), F32)
    grid_spec = pltpu.PrefetchScalarGridSpec(
        num_scalar_prefetch=2,
        grid=(nh, int(qi.shape[0])),
        in_specs=[pl.BlockSpec((1, tq, dk), lambda h, t, qi, kj: (h, qi[t], 0)),
                  pl.BlockSpec((1, tk, dk), lambda h, t, qi, kj: (h, kj[t], 0)),
                  pl.BlockSpec((1, tk, dv), lambda h, t, qi, kj: (h, kj[t], 0))],
        out_specs=pl.BlockSpec((1, tq, dv), lambda h, t, qi, kj: (h, qi[t], 0)),
        scratch_shapes=[pltpu.VMEM((tq, 1), F32), pltpu.VMEM((tq, 1), F32), pltpu.VMEM((tq, dv), F32)],
    )
    return pl.pallas_call(
        functools.partial(_flash_kernel, tq=tq, tk=tk),
        out_shape=jax.ShapeDtypeStruct((nh, s, dv), BF16),
        grid_spec=grid_spec,
        compiler_params=_params(("parallel", "arbitrary"), vm),
        name="mla_flash_attention",
    )(qi, kj, q, k, v)


def _spatial_kernel(w_ref, bias_ref, v_ref, u_ref, o_ref, *, nch):
    c = CHUNK
    row = lax.broadcasted_iota(jnp.int32, (c, c), 0)
    col = lax.broadcasted_iota(jnp.int32, (c, c), 1)
    w = jnp.where(col <= row, w_ref[0], 0.0).astype(BF16)
    bias = bias_ref[0]
    for ci in range(nch):
        sl = pl.ds(ci * c, c)
        sg = jnp.dot(w, v_ref[sl, :], preferred_element_type=F32) + bias
        o_ref[sl, :] = (u_ref[sl, :].astype(F32) * sg).astype(o_ref.dtype)


def spatial_gate(z, vn, w_sp, b_sp):
    s, d = vn.shape
    g = w_sp.shape[0]
    gd = d // g
    nch = _pick(s // CHUNK, (16, 8, 4, 2, 1))
    rows = nch * CHUNK
    bias = jnp.broadcast_to(b_sp.astype(F32)[:, :, None], (g, CHUNK, gd))
    return pl.pallas_call(
        functools.partial(_spatial_kernel, nch=nch),
        out_shape=jax.ShapeDtypeStruct((s, d), BF16),
        grid=(g, s // rows),
        in_specs=[pl.BlockSpec((1, CHUNK, CHUNK), lambda gi, ci: (gi, 0, 0)),
                  pl.BlockSpec((1, CHUNK, gd), lambda gi, ci: (gi, 0, 0)),
                  pl.BlockSpec((rows, gd), lambda gi, ci: (ci, gi)),
                  pl.BlockSpec((rows, gd), lambda gi, ci: (ci, gi))],
        out_specs=pl.BlockSpec((rows, gd), lambda gi, ci: (ci, gi)),
        compiler_params=_params(("parallel", "parallel"), 16 << 20),
        name="spatial_gate",
    )(w_sp.astype(F32), bias, vn, z)


def _norm_router_kernel(h_ref, g_ref, wr_ref, xn_ref, lg_ref, *, eps):
    x = h_ref[...]
    ms = jnp.mean(x * x, axis=-1, keepdims=True)
    xn = x * lax.rsqrt(ms + eps) * g_ref[...]
    x_hi = xn.astype(BF16)
    xn_ref[...] = x_hi
    x_lo = (xn - x_hi.astype(F32)).astype(BF16)
    w = wr_ref[...]
    w_hi = w.astype(BF16)
    w_lo = (w - w_hi.astype(F32)).astype(BF16)
    lg_ref[...] = (jnp.dot(x_hi, w_hi, preferred_element_type=F32) + jnp.dot(x_hi, w_lo, preferred_element_type=F32)
                   + jnp.dot(x_lo, w_hi, preferred_element_type=F32))


def norm_router(h, g, w_router):
    m, d = h.shape
    e = w_router.shape[1]
    wr = jnp.pad(w_router.astype(F32), ((0, 0), (0, LANES - e)))
    tm = _pick(m, (512, 256, 128, 64, 32, 16, 8))
    vm = 2 * (_nbytes((tm, d), F32) + _nbytes((tm, d), BF16) + _nbytes((d, LANES), F32)) + 6 * _nbytes((tm, d), F32)
    xn, lg = pl.pallas_call(
        functools.partial(_norm_router_kernel, eps=RMS_EPS),
        out_shape=(jax.ShapeDtypeStruct((m, d), BF16), jax.ShapeDtypeStruct((m, LANES), F32)),
        grid=(m // tm,),
        in_specs=[pl.BlockSpec((tm, d), lambda i: (i, 0)),
                  pl.BlockSpec((1, d), lambda i: (0, 0)),
                  pl.BlockSpec((d, LANES), lambda i: (0, 0))],
        out_specs=(pl.BlockSpec((tm, d), lambda i: (i, 0)), pl.BlockSpec((tm, LANES), lambda i: (i, 0))),
        compiler_params=_params(("parallel",), vm),
        name="norm_router",
    )(h, g.reshape(1, d).astype(F32), wr)
    return xn, lg[:, :e]


def _make_epi_moe(tn, f_expert):
    def epi(accs, extras):
        g, u = accs
        gates = extras[0]
        e = (pl.program_id(1) * tn) // f_expert
        lane = lax.broadcasted_iota(jnp.int32, gates.shape, 1)
        gate = jnp.sum(jnp.where(lane == e, gates, 0.0), axis=1, keepdims=True)
        return gate * (g * jax.nn.sigmoid(g) * u)
    return epi


def _token_shift(y):
    return jnp.pad(y, ((1, 0), (0, 0)))[:-1]


def _heads_major(t, nh):
    s = t.shape[0]
    return jnp.transpose(t.reshape(s, nh, -1), (1, 0, 2))


def _rwkv_group(y, col0, mu, w0, w2, a0, a2, g2, k_k, k_a, r_k, lnx_w, lnx_b):
    s = y.shape[0]
    rkv = y[:, :3 * RW_DIM].astype(F32)
    lo = y[:, col0:col0 + DECAY_LORA + AAA_LORA + GATE_LORA].astype(F32)
    mu_rkv = mu[:3 * RW_DIM]
    mu_lo = mu[3 * RW_DIM:]
    rkv = rkv + (_token_shift(rkv) - rkv) * mu_rkv
    lo = lo + (_token_shift(lo) - lo) * mu_lo
    r, k, v = rkv[:, :RW_DIM], rkv[:, RW_DIM:2 * RW_DIM], rkv[:, 2 * RW_DIM:]
    w_lo = lo[:, :DECAY_LORA]
    a_lo = lo[:, DECAY_LORA:DECAY_LORA + AAA_LORA]
    g_lo = lo[:, DECAY_LORA + AAA_LORA:]
    w_pre = matmul(jnp.tanh(w_lo).astype(BF16), [w2.astype(BF16)], _epi_id, F32, tm=2048, tn=1024, name="rw_decay_lora")
    a_pre = matmul(a_lo.astype(BF16), [a2.astype(BF16)], _epi_id, F32, tm=2048, tn=1024, name="rw_a_lora")
    g = matmul(jax.nn.sigmoid(g_lo).astype(BF16), [g2.astype(BF16)], _epi_id, F32, tm=2048, tn=1024, name="rw_gate_lora")
    w_log = -jax.nn.softplus(-(w0 + w_pre)) - 0.5
    lw = -jnp.exp(w_log)
    a = jax.nn.sigmoid(a0 + a_pre)
    hd = lambda t: t.reshape(s, RW_HEADS, RW_HEAD)
    r3, k3, v3, a3, lw3 = hd(r), hd(k), hd(v), hd(a), hd(lw)
    kk = k3 * k_k.reshape(RW_HEADS, RW_HEAD)
    kk = kk * lax.rsqrt(jnp.maximum(jnp.sum(kk * kk, axis=-1, keepdims=True), 1e-24))
    k3 = k3 * (1.0 + (a3 - 1.0) * k_a.reshape(RW_HEADS, RW_HEAD))
    tr = lambda t: jnp.transpose(t, (1, 0, 2))
    o = rwkv_recurrence(tr(r3), tr(k3), tr(v3), tr(-kk), tr(kk * a3), tr(lw3))
    o = jnp.transpose(o, (1, 0, 2))
    mean = jnp.mean(o, axis=-1, keepdims=True)
    var = jnp.mean(jnp.square(o - mean), axis=-1, keepdims=True)
    o = ((o - mean) * lax.rsqrt(var + GN_EPS)).reshape(s, RW_DIM)
    o = o * lnx_w + lnx_b
    bonus = jnp.sum(r3 * k3 * r_k, axis=-1, keepdims=True) * v3
    o = (o + bonus.reshape(s, RW_DIM)) * g
    return o.astype(BF16)


def _rope(x, cos, sin):
    half = x.shape[-1] // 2
    x1, x2 = x[..., :half], x[..., half:]
    return jnp.concatenate([x1 * cos - x2 * sin, x2 * cos + x1 * sin], axis=-1)


def _mla_group(y, col_q, col_kv, col_pe, cos, sin, q_norm, w_uq, kv_norm, w_ukv):
    s = y.shape[0]
    nh = MLA_HEADS
    cqn = rmsnorm(y, q_norm, BF16, width=Q_LORA, col_block=col_q // Q_LORA)
    ckvn = rmsnorm(y, kv_norm, BF16, width=KV_LORA, col_block=col_kv // KV_LORA)
    q = matmul(cqn, [w_uq.astype(BF16)], _epi_id, F32, tm=1024, tn=512, name="mla_q_up")
    kv = matmul(ckvn, [w_ukv.astype(BF16)], _epi_id, BF16, tm=1024, tn=512, name="mla_kv_up")
    q = q.reshape(s, nh, QK_NOPE + QK_ROPE)
    q_nope, q_pe = q[..., :QK_NOPE], q[..., QK_NOPE:]
    q_pe = _rope(q_pe, cos[:, None, :], sin[:, None, :])
    scale = (QK_NOPE + QK_ROPE) ** -0.5
    qf = (jnp.concatenate([q_nope, q_pe], axis=-1) * scale).astype(BF16)
    kv = kv.reshape(s, nh, QK_NOPE + V_HEAD)
    k_nope, v = kv[..., :QK_NOPE], kv[..., QK_NOPE:]
    k_pe = _rope(y[:, col_pe:col_pe + QK_ROPE].astype(F32), cos, sin).astype(BF16)
    kf = jnp.concatenate([k_nope, jnp.broadcast_to(k_pe[:, None, :], (s, nh, QK_ROPE))], axis=-1)
    tr = lambda t: jnp.transpose(t, (1, 0, 2))
    o = flash_attention(tr(qf), tr(kf), tr(v))
    return jnp.transpose(o, (1, 0, 2)).reshape(s, nh * V_HEAD)


def _even_layer(h, hn, j, cos, sin, prm):
    w_in = prm["ev_in_proj"][j]
    rw_end = 3 * RW_DIM
    lora_w = DECAY_LORA + AAA_LORA + GATE_LORA
    rw_proj = rw_end + lora_w
    w_perm = jnp.concatenate([
        w_in[:, :rw_end],
        w_in[:, rw_proj:rw_proj + Q_LORA + KV_LORA],
        w_in[:, rw_end:rw_proj],
        w_in[:, rw_proj + Q_LORA + KV_LORA:],
    ], axis=1).astype(BF16)
    y = matmul(hn, [w_perm], _epi_id, BF16, tm=1024, tn=512, name="even_in_proj")
    col_q = rw_end
    col_kv = rw_end + Q_LORA
    col_lo = rw_end + Q_LORA + KV_LORA
    col_pe = col_lo + lora_w
    mu = prm["rw_mu"][j]
    o_rw = _rwkv_group(y, col_lo, mu, prm["rw_w0"][j], prm["rw_w2"][j], prm["rw_a0"][j], prm["rw_a2"][j],
                       prm["rw_g2"][j], prm["rw_k_k"][j], prm["rw_k_a"][j], prm["rw_r_k"][j],
                       prm["rw_lnx_w"][j], prm["rw_lnx_b"][j])
    o_mla = _mla_group(y, col_q, col_kv, col_pe, cos, sin, prm["mla_q_norm"][j], prm["mla_w_uq"][j],
                       prm["mla_kv_norm"][j], prm["mla_w_ukv"][j])
    mix = jnp.concatenate([o_rw, o_mla], axis=-1)
    tm, tn = 1024, 512
    h = matmul(mix, [prm["ev_out_proj"][j].astype(BF16)], _epi_residual, F32, tm=tm, tn=tn,
               extras=[_res_extra(h, _pick(h.shape[0], (tm, 512, 256, 128, 64, 32, 16, 8)),
                                  _pick(h.shape[1], (tn, 512, 256, 128)), 1)], name="even_out_proj")
    return h


def _dense_ffn(h, hn, wg, wu, wd):
    d, f = wg.shape
    fp = -(-f // 1024) * 1024 if f > 1024 else f
    pad = fp - f
    wg = jnp.pad(wg.astype(BF16), ((0, 0), (0, pad)))
    wu = jnp.pad(wu.astype(BF16), ((0, 0), (0, pad)))
    wd = jnp.pad(wd.astype(BF16), ((0, pad), (0, 0)))
    mid = matmul(hn, [wg, wu], _epi_swiglu, BF16, tm=1024, tn=512, name="ffn_up")
    tm = _pick(h.shape[0], (1024, 512, 256, 128, 64, 32, 16, 8))
    tn = _pick(h.shape[1], (1024, 512, 256, 128))
    tk = _pick(fp, (2816, 2048, 1024, 512, 256, 128))
    nk = fp // tk
    return matmul(mid, [wd], _epi_residual, F32, tm=tm, tn=tn, tk=tk,
                  extras=[_res_extra(h, tm, tn, nk)], name="ffn_down")


def _moe(h, g_norm, w_router, wg, wu, wd):
    ne, d, f = wg.shape
    xn, logits = norm_router(h, g_norm, w_router)
    top_val, top_idx = lax.top_k(logits, TOP_K)
    top_w = jax.nn.softmax(top_val, axis=-1)
    gates = jnp.sum(jax.nn.one_hot(top_idx, ne, dtype=F32) * top_w[..., None], axis=-2)
    m = h.shape[0]
    tm = _pick(m, (1024, 512, 256, 128, 64, 32, 16, 8))
    tn = _pick(f, (512, 256, 128))
    nt = f // tn
    w_spec = pl.BlockSpec((None, d, tn), lambda i, jn: (jn // nt, 0, jn % nt))
    mid = matmul(xn, [wg.astype(BF16), wu.astype(BF16)], _make_epi_moe(tn, f), BF16, tm=tm, tn=tn,
                 w_specs=[w_spec, w_spec], n_out=ne * f,
                 extras=[(gates, (tm, ne), lambda i, jn: (i, 0))], name="moe_up")
    wd2 = wd.astype(BF16).reshape(ne * f, d)
    tn2 = _pick(d, (1024, 512, 256, 128))
    tk = _pick(ne * f, (3072, 2048, 1024, 512, 256, 128))
    nk = ne * f // tk
    return matmul(mid, [wd2], _epi_residual, F32, tm=tm, tn=tn2, tk=tk,
                  extras=[_res_extra(h, tm, tn2, nk)], name="moe_down")


def _odd_layer(h, hn, j, prm):
    z = matmul(hn, [prm["sg_in_proj"][j].astype(BF16)], _epi_gelu, BF16, tm=1024, tn=512, name="sg_in_proj")
    d = h.shape[1]
    vf = z[:, d:].astype(F32)
    mean = jnp.mean(vf, axis=-1, keepdims=True)
    var = jnp.mean(jnp.square(vf - mean), axis=-1, keepdims=True)
    vn = ((vf - mean) * lax.rsqrt(var + LN_EPS) * prm["sg_ln_w"][j] + prm["sg_ln_b"][j]).astype(BF16)
    gated = spatial_gate(z, vn, prm["sg_w_spatial"][j], prm["sg_b_spatial"][j])
    tm = _pick(h.shape[0], (1024, 512, 256, 128, 64, 32, 16, 8))
    tn = _pick(d, (512, 256, 128))
    return matmul(gated, [prm["sg_out_proj"][j].astype(BF16)], _epi_residual, F32, tm=tm, tn=tn,
                  extras=[_res_extra(h, tm, tn, 1)], name="sg_out_proj")


def kernel(x, p, positions, norm_mix, norm_ffn, norm_ple, final_norm, ev_in_proj, rw_mu, rw_w0, rw_w2, rw_a0, rw_a2, rw_g2, rw_k_k, rw_k_a, rw_r_k, rw_lnx_w, rw_lnx_b, mla_q_norm, mla_w_uq, mla_kv_norm, mla_w_ukv, ev_out_proj, ffn_w_gate, ffn_w_up, ffn_w_down, sg_in_proj, sg_ln_w, sg_ln_b, sg_w_spatial, sg_b_spatial, sg_out_proj, moe_router, moe_w_gate, moe_w_up, moe_w_down, ple_proj, ple_gate):
    prm = dict(ev_in_proj=ev_in_proj, rw_mu=rw_mu, rw_w0=rw_w0, rw_w2=rw_w2, rw_a0=rw_a0, rw_a2=rw_a2, rw_g2=rw_g2,
               rw_k_k=rw_k_k, rw_k_a=rw_k_a, rw_r_k=rw_r_k, rw_lnx_w=rw_lnx_w, rw_lnx_b=rw_lnx_b,
               mla_q_norm=mla_q_norm, mla_w_uq=mla_w_uq, mla_kv_norm=mla_kv_norm, mla_w_ukv=mla_w_ukv,
               ev_out_proj=ev_out_proj, sg_in_proj=sg_in_proj, sg_ln_w=sg_ln_w, sg_ln_b=sg_ln_b,
               sg_w_spatial=sg_w_spatial, sg_b_spatial=sg_b_spatial, sg_out_proj=sg_out_proj)
    bsz, s, d = x.shape
    assert bsz == 1
    inv = 1.0 / (ROPE_THETA ** (jnp.arange(0, QK_ROPE, 2, dtype=F32) / QK_ROPE))
    ang = positions[0].astype(F32)[:, None] * inv
    cos, sin = jnp.cos(ang), jnp.sin(ang)
    h = x[0]
    for i in range(DEPTH):
        j = i // 2
        if i % 2 == 0:
            hn = rmsnorm(h, norm_mix[i], BF16)
            h = _even_layer(h, hn, j, cos, sin, prm)
            hn = rmsnorm(h, norm_ffn[i], BF16)
            h = _dense_ffn(h, hn, ffn_w_gate[j], ffn_w_up[j], ffn_w_down[j])
        else:
            hn = rmsnorm(h, norm_mix[i], BF16)
            h = _odd_layer(h, hn, j, prm)
            h = _moe(h, norm_ffn[i], moe_router[j], moe_w_gate[j], moe_w_up[j], moe_w_down[j])
        hn = rmsnorm(h, norm_ple[i], BF16)
        h = ple(hn, ple_gate[i].astype(BF16), p[i, 0].astype(BF16), ple_proj[i].astype(BF16), h)
    return rmsnorm(h, final_norm, F32)[None]
```

```python
import functools

import jax
import jax.numpy as jnp
from jax import lax
from jax.experimental import pallas as pl
from jax.experimental.pallas import tpu as pltpu

D_MODEL = 4096
DEPTH = 4
RW_HEADS = 32
RW_HEAD = 64
RW_DIM = RW_HEADS * RW_HEAD
DECAY_LORA = 96
AAA_LORA = 96
GATE_LORA = 256
GN_EPS = 64e-5
MLA_HEADS = 16
Q_LORA = 1536
KV_LORA = 512
QK_NOPE = 128
QK_ROPE = 64
V_HEAD = 128
ROPE_THETA = 10000.0
CHUNK = 128
SG_GROUPS = 32
LN_EPS = 1e-5
D_FF = 11008
N_EXPERTS = 8
TOP_K = 2
D_FF_EXPERT = 1536
PLE_DIM = 256
RMS_EPS = 1e-6

V7X_VMEM_BYTES = 64 * 1024 * 1024
LANES = 128
RW_CHUNK = 64
RW_HEADS_PER_STEP = 8
RW_CHUNKS_PER_STEP = 2
FLASH_T = 1024

F32 = jnp.float32
BF16 = jnp.bfloat16


def _pick(dim, candidates):
    for c in candidates:
        if dim % c == 0:
            return c
    return dim


def _params(sem, vmem_bytes, flags=None):
    limit = int(min(V7X_VMEM_BYTES - 6 * 1024 * 1024, max(vmem_bytes * 5 // 4 + (2 << 20), 16 << 20)))
    return pltpu.CompilerParams(dimension_semantics=sem, vmem_limit_bytes=limit, flags=flags)


def _nbytes(shape, dtype):
    n = 1
    for s in shape:
        n *= s
    return n * jnp.dtype(dtype).itemsize


def _rmsnorm_kernel(x_ref, g_ref, o_ref, *, eps):
    x = x_ref[...].astype(F32)
    ms = jnp.mean(x * x, axis=-1, keepdims=True)
    o_ref[...] = (x * lax.rsqrt(ms + eps) * g_ref[...]).astype(o_ref.dtype)


def rmsnorm(x, g, out_dtype, *, width=None, col_block=0):
    m = x.shape[0]
    d = x.shape[1] if width is None else width
    tm = _pick(m, (512, 256, 128, 64, 32, 16, 8))
    vm = 2 * (_nbytes((tm, d), x.dtype) + _nbytes((tm, d), out_dtype)) + 4 * _nbytes((tm, d), F32)
    return pl.pallas_call(
        functools.partial(_rmsnorm_kernel, eps=RMS_EPS),
        out_shape=jax.ShapeDtypeStruct((m, d), out_dtype),
        grid=(m // tm,),
        in_specs=[pl.BlockSpec((tm, d), lambda i: (i, col_block)),
                  pl.BlockSpec((1, d), lambda i: (0, 0))],
        out_specs=pl.BlockSpec((tm, d), lambda i: (i, 0)),
        compiler_params=_params(("parallel",), vm),
        name="rmsnorm",
    )(x, g.reshape(1, d).astype(F32))


def _layernorm_kernel(x_ref, w_ref, b_ref, o_ref, *, eps):
    x = x_ref[...].astype(F32)
    mean = jnp.mean(x, axis=-1, keepdims=True)
    var = jnp.mean(jnp.square(x - mean), axis=-1, keepdims=True)
    o_ref[...] = ((x - mean) * lax.rsqrt(var + eps) * w_ref[...] + b_ref[...]).astype(o_ref.dtype)


def layernorm(x, w, b, out_dtype, *, width, col_block):
    m = x.shape[0]
    tm = _pick(m, (512, 256, 128, 64, 32, 16, 8))
    vm = 2 * (_nbytes((tm, width), x.dtype) + _nbytes((tm, width), out_dtype)) + 4 * _nbytes((tm, width), F32)
    return pl.pallas_call(
        functools.partial(_layernorm_kernel, eps=LN_EPS),
        out_shape=jax.ShapeDtypeStruct((m, width), out_dtype),
        grid=(m // tm,),
        in_specs=[pl.BlockSpec((tm, width), lambda i: (i, col_block)),
                  pl.BlockSpec((1, width), lambda i: (0, 0)),
                  pl.BlockSpec((1, width), lambda i: (0, 0))],
        out_specs=pl.BlockSpec((tm, width), lambda i: (i, 0)),
        compiler_params=_params(("parallel",), vm),
        name="layernorm",
    )(x, w.reshape(1, width).astype(F32), b.reshape(1, width).astype(F32))


def _mm_kernel(*refs, nk, n_w, n_extra, epi):
    a_ref = refs[0]
    w_refs = refs[1:1 + n_w]
    extra_refs = refs[1 + n_w:1 + n_w + n_extra]
    o_ref = refs[1 + n_w + n_extra]
    acc_refs = refs[2 + n_w + n_extra:]
    a = a_ref[...]
    if nk == 1:
        accs = [jnp.dot(a, w[...], preferred_element_type=F32) for w in w_refs]
        o_ref[...] = epi(accs, [e[...] for e in extra_refs]).astype(o_ref.dtype)
        return
    k = pl.program_id(2)

    @pl.when(k == 0)
    def _():
        for acc in acc_refs:
            acc[...] = jnp.zeros_like(acc)

    for acc, w in zip(acc_refs, w_refs):
        acc[...] += jnp.dot(a, w[...], preferred_element_type=F32)

    @pl.when(k == nk - 1)
    def _():
        o_ref[...] = epi([acc[...] for acc in acc_refs], [e[...] for e in extra_refs]).astype(o_ref.dtype)


def matmul(a, ws, epi, out_dtype, *, extras=(), tm=1024, tn=512, tk=None, a_col_block=0, a_width=None,
           w_specs=None, n_out=None, name="matmul"):
    m = a.shape[0]
    kdim = a.shape[1] if a_width is None else a_width
    n = ws[0].shape[-1] if n_out is None else n_out
    tm = _pick(m, (tm, 512, 256, 128, 64, 32, 16, 8))
    tn = _pick(n, (tn, 512, 256, 128))
    tk = kdim if tk is None else _pick(kdim, (tk,))
    nk = kdim // tk
    kb0 = a_col_block * nk
    if nk == 1:
        grid = (m // tm, n // tn)
        a_spec = pl.BlockSpec((tm, tk), lambda i, j: (i, kb0))
        w_spec_list = w_specs or [pl.BlockSpec((tk, tn), lambda i, j: (0, j)) for _ in ws]
        o_spec = pl.BlockSpec((tm, tn), lambda i, j: (i, j))
        sem = ("parallel", "parallel")
        scratch = []
    else:
        grid = (m // tm, n // tn, nk)
        a_spec = pl.BlockSpec((tm, tk), lambda i, j, k: (i, kb0 + k))
        w_spec_list = w_specs or [pl.BlockSpec((tk, tn), lambda i, j, k: (k, j)) for _ in ws]
        o_spec = pl.BlockSpec((tm, tn), lambda i, j, k: (i, j))
        sem = ("parallel", "parallel", "arbitrary")
        scratch = [pltpu.VMEM((tm, tn), F32) for _ in ws]
    extra_specs = [pl.BlockSpec(bs, im) for (_, bs, im) in extras]
    vm = 2 * _nbytes((tm, tk), a.dtype) + 2 * len(ws) * _nbytes((tk, tn), ws[0].dtype)
    vm += 2 * _nbytes((tm, tn), out_dtype) + (len(ws) + 2) * _nbytes((tm, tn), F32)
    for (arr, bs, _) in extras:
        vm += 2 * _nbytes([1 if b is None else b for b in bs], arr.dtype)
    return pl.pallas_call(
        functools.partial(_mm_kernel, nk=nk, n_w=len(ws), n_extra=len(extras), epi=epi),
        out_shape=jax.ShapeDtypeStruct((m, n), out_dtype),
        grid=grid,
        in_specs=[a_spec] + list(w_spec_list) + extra_specs,
        out_specs=o_spec,
        scratch_shapes=scratch,
        compiler_params=_params(sem, vm),
        name=name,
    )(a, *ws, *[e[0] for e in extras])


def _epi_id(accs, extras):
    return accs[0]


def _epi_residual(accs, extras):
    return extras[0] + accs[0]


def _epi_swiglu(accs, extras):
    g, u = accs
    return g * jax.nn.sigmoid(g) * u


def _epi_gelu(accs, extras):
    z = accs[0]
    return 0.5 * z * (1.0 + lax.erf(z * (2.0 ** -0.5)))


def _res_extra(h, tm, tn, nk):
    if nk == 1:
        return (h, (tm, tn), lambda i, j: (i, j))
    return (h, (tm, tn), lambda i, j, k: (i, j))


def _ple_kernel(hn_ref, wg_ref, p_ref, wp_ref, h_ref, o_ref):
    gate = jax.nn.sigmoid(jnp.dot(hn_ref[...], wg_ref[...], preferred_element_type=F32))
    proj = jnp.dot(p_ref[...], wp_ref[...], preferred_element_type=F32)
    o_ref[...] = h_ref[...] + gate * proj


def ple(hn, w_gate, p, w_proj, h):
    m, d = hn.shape
    n = w_gate.shape[1]
    pd = p.shape[1]
    tm = _pick(m, (1024, 512, 256, 128, 64, 32, 16, 8))
    tn = _pick(n, (512, 256, 128))
    vm = 2 * (_nbytes((tm, d), BF16) + _nbytes((d, tn), BF16) + _nbytes((tm, pd), BF16) + _nbytes((pd, tn), BF16))
    vm += 6 * _nbytes((tm, tn), F32)
    return pl.pallas_call(
        _ple_kernel,
        out_shape=jax.ShapeDtypeStruct((m, n), F32),
        grid=(m // tm, n // tn),
        in_specs=[pl.BlockSpec((tm, d), lambda i, j: (i, 0)),
                  pl.BlockSpec((d, tn), lambda i, j: (0, j)),
                  pl.BlockSpec((tm, pd), lambda i, j: (i, 0)),
                  pl.BlockSpec((pd, tn), lambda i, j: (0, j)),
                  pl.BlockSpec((tm, tn), lambda i, j: (i, j))],
        out_specs=pl.BlockSpec((tm, tn), lambda i, j: (i, j)),
        compiler_params=_params(("parallel", "parallel"), vm),
        name="ple",
    )(hn, w_gate, p, w_proj, h)


def _bdot(a, b):
    return jnp.dot(a.astype(BF16), b.astype(BF16), preferred_element_type=F32)


def _bdot_nt(a, b):
    return lax.dot_general(a.astype(BF16), b.astype(BF16), (((1,), (1,)), ((), ())),
                           preferred_element_type=F32)


def _bdot_tn(a, b):
    return lax.dot_general(a.astype(BF16), b.astype(BF16), (((0,), (0,)), ((), ())),
                           preferred_element_type=F32)


def _split3(x):
    hi = x.astype(BF16)
    r1 = x - hi.astype(F32)
    mid = r1.astype(BF16)
    lo = (r1 - mid.astype(F32)).astype(BF16)
    return hi, mid, lo


def _bmm(a, b):
    return jnp.einsum("bij,bjk->bik", a.astype(BF16), b.astype(BF16), preferred_element_type=F32)


def _bmm_nt(a, b):
    return jnp.einsum("bik,bjk->bij", a.astype(BF16), b.astype(BF16), preferred_element_type=F32)


def _bmm_tn(a, b):
    return jnp.einsum("bti,btj->bij", a.astype(BF16), b.astype(BF16), preferred_element_type=F32)


def _rwkv_a_kernel(r_ref, k_ref, v_ref, ai_ref, lw_ref, kk_ref, ka_ref, mt_ref, nt_ref, q_ref, o0_ref, *, hb, ncb):
    t = RW_CHUNK
    n = RW_HEAD
    nb = hb * ncb
    row = lax.broadcasted_iota(jnp.int32, (t, t), 0)
    col = lax.broadcasted_iota(jnp.int32, (t, t), 1)
    incl = col <= row
    strict = col < row
    tri = jnp.broadcast_to(incl.astype(BF16), (nb, t, t))
    eye = (row == col).astype(F32)
    drow = lax.broadcasted_iota(jnp.int32, (n, n), 0)
    dcol = lax.broadcasted_iota(jnp.int32, (n, n), 1)

    def units(ref):
        return jnp.stack([ref[c * t:(c + 1) * t, h * n:(h + 1) * n] for c in range(ncb) for h in range(hb)])

    def unit_params(ref):
        return jnp.stack([ref[:, h * n:(h + 1) * n] for c in range(ncb) for h in range(hb)])

    k = units(k_ref)
    ai = units(ai_ref)
    kk = k * unit_params(kk_ref)
    kk = kk * lax.rsqrt(jnp.maximum(jnp.sum(kk * kk, axis=-1, keepdims=True), 1e-24))
    k = k * (1.0 + (ai - 1.0) * unit_params(ka_ref))
    a = -kk
    b = kk * ai
    lw = units(lw_ref)
    hi, mid, lo = _split3(lw)
    cum = (jnp.einsum("bij,bjk->bik", tri, hi, preferred_element_type=F32)
           + jnp.einsum("bij,bjk->bik", tri, mid, preferred_element_type=F32)
           + jnp.einsum("bij,bjk->bik", tri, lo, preferred_element_type=F32))
    cum_t = cum[:, t - 1:t, :]
    e_in = jnp.exp(cum)
    e_out = jnp.exp(-cum)
    e_prev = jnp.exp(cum - lw)
    e_rem = jnp.exp(cum_t - cum)
    p_t = jnp.exp(cum_t)
    r = units(r_ref)
    v = units(v_ref)
    rh = r * e_in
    ah = a * e_prev
    bh = b * e_out
    kh = k * e_out
    bt = b * e_rem
    kt = k * e_rem
    n_ab = jnp.where(strict, _bmm_nt(ah, bh), 0.0)
    n_ak = jnp.where(strict, _bmm_nt(ah, kh), 0.0)
    l_rb = jnp.where(incl, _bmm_nt(rh, bh), 0.0)
    l_rk = jnp.where(incl, _bmm_nt(rh, kh), 0.0)
    x = eye + n_ab
    pw = n_ab
    for _ in range((t - 1).bit_length() - 1):
        pw = _bmm(pw, pw)
        x = x + _bmm(x, pw)
    akv = _bmm(n_ak, v)
    w = _bmm(x, ah)
    u0 = _bmm(x, akv)
    qt = rh + _bmm(l_rb, w)
    o0 = _bmm(l_rb, u0) + _bmm(l_rk, v)
    mt = jnp.where(drow == dcol, p_t, 0.0) + _bmm_tn(bt, w)
    nt = _bmm_tn(bt, u0) + _bmm_tn(kt, v)
    for c in range(ncb):
        rows = slice(c * t, (c + 1) * t)
        q_ref[rows, :] = jnp.concatenate([qt[c * hb + h] for h in range(hb)], axis=-1)
        o0_ref[rows, :] = jnp.concatenate([o0[c * hb + h] for h in range(hb)], axis=-1)
        mt_ref[:, c] = mt[c * hb:(c + 1) * hb]
        nt_ref[:, c] = nt[c * hb:(c + 1) * hb]


def _bdot3(a, b):
    a_hi = a.astype(BF16)
    a_lo = (a - a_hi.astype(F32)).astype(BF16)
    b_hi = b.astype(BF16)
    b_lo = (b - b_hi.astype(F32)).astype(BF16)
    mm = lambda x, y: jnp.einsum("bij,bjk->bik", x, y, preferred_element_type=F32)
    return mm(a_hi, b_hi) + mm(a_hi, b_lo) + mm(a_lo, b_hi)


def _rwkv_b_kernel(mt_ref, nt_ref, q_ref, o0_ref, r_ref, k_ref, v_ref, ai_ref, g_ref, ka_ref, rk_ref, lw_ref, lb_ref,
                   o_ref, z_ref, *, nh):
    n = RW_HEAD
    c = pl.program_id(0)

    @pl.when(c == 0)
    def _():
        z_ref[...] = jnp.zeros_like(z_ref)

    def heads(ref):
        return jnp.stack([ref[:, h * n:(h + 1) * n] for h in range(nh)])

    z = z_ref[...]
    o = heads(o0_ref) + _bdot3(heads(q_ref), z)
    z_ref[...] = nt_ref[:, 0] + _bdot3(mt_ref[:, 0], z)
    mean = jnp.mean(o, axis=-1, keepdims=True)
    var = jnp.mean(jnp.square(o - mean), axis=-1, keepdims=True)
    o = (o - mean) * lax.rsqrt(var + GN_EPS) * heads(lw_ref) + heads(lb_ref)
    k = heads(k_ref) * (1.0 + (heads(ai_ref) - 1.0) * heads(ka_ref))
    bonus = jnp.sum(heads(r_ref) * k * heads(rk_ref), axis=-1, keepdims=True) * heads(v_ref)
    out = (o + bonus) * heads(g_ref)
    o_ref[...] = jnp.concatenate([out[h] for h in range(nh)], axis=-1).astype(o_ref.dtype)


def rwkv_mix(r, k, v, ai, lw, g, k_k, k_a, r_k, lnx_w, lnx_b):
    s, dim = r.shape
    n = RW_HEAD
    nh = dim // n
    t = RW_CHUNK
    nc = s // t
    hb = _pick(nh, (RW_HEADS_PER_STEP, 4, 2))
    ncb = _pick(nc, (RW_CHUNKS_PER_STEP, 2, 1))
    row = lambda p: p.reshape(1, dim).astype(F32)
    seq_spec = pl.BlockSpec((ncb * t, hb * n), lambda g_, c: (c, g_))
    par_spec = pl.BlockSpec((1, hb * n), lambda g_, c: (0, g_))
    mat_spec = pl.BlockSpec((hb, ncb, n, n), lambda g_, c: (g_, c, 0, 0))
    mt, nt, qt, o0 = pl.pallas_call(
        functools.partial(_rwkv_a_kernel, hb=hb, ncb=ncb),
        out_shape=(jax.ShapeDtypeStruct((nh, nc, n, n), F32), jax.ShapeDtypeStruct((nh, nc, n, n), F32),
                   jax.ShapeDtypeStruct((s, dim), F32), jax.ShapeDtypeStruct((s, dim), F32)),
        grid=(nh // hb, nc // ncb),
        in_specs=[seq_spec] * 5 + [par_spec] * 2,
        out_specs=(mat_spec, mat_spec, seq_spec, seq_spec),
        compiler_params=_params(("parallel", "parallel"), 32 << 20),
        name="rwkv_chunk_local",
    )(r, k, v, ai, lw, row(k_k), row(k_a))
    seq_all = pl.BlockSpec((t, dim), lambda c: (c, 0))
    par_all = pl.BlockSpec((1, dim), lambda c: (0, 0))
    mat_all = pl.BlockSpec((nh, 1, n, n), lambda c: (0, c, 0, 0))
    return pl.pallas_call(
        functools.partial(_rwkv_b_kernel, nh=nh),
        out_shape=jax.ShapeDtypeStruct((s, dim), BF16),
        grid=(nc,),
        in_specs=[mat_all, mat_all] + [seq_all] * 7 + [par_all] * 4,
        out_specs=seq_all,
        scratch_shapes=[pltpu.VMEM((nh, n, n), F32)],
        compiler_params=_params(("arbitrary",), 32 << 20),
        name="rwkv_state_scan",
    )(mt, nt, qt, o0, r, k, v, ai, g, row(k_a), row(r_k), row(lnx_w), row(lnx_b))


def _flash_kernel(qn_ref, qp_ref, kn_ref, kp_ref, v_ref, o_ref, q_sc, m_sc, acc_sc, s0_sc, s1_sc, *, t):
    i = pl.program_id(1)
    nt = t // LANES
    q_sc[...] = jnp.concatenate([qn_ref[...], qp_ref[...]], axis=1)
    m_sc[...] = jnp.full_like(m_sc, -1e30)
    acc_sc[...] = jnp.zeros_like(acc_sc)
    ones = jnp.ones((t, LANES), BF16)

    def scores(j, s_ref):
        start = pl.multiple_of(j * t, t)
        kb = jnp.concatenate([kn_ref[pl.ds(start, t), :], kp_ref[pl.ds(start, t), :]], axis=1)
        s_ref[...] = lax.dot_general(q_sc[...], kb, (((1,), (1,)), ((), ())), preferred_element_type=F32)

    def softmax_pv(j, s_ref, masked):
        start = pl.multiple_of(j * t, t)
        vb = jnp.concatenate([v_ref[pl.ds(start, t), :], ones], axis=1)
        tiles = [s_ref[:, c * LANES:(c + 1) * LANES] for c in range(nt)]
        if masked:
            rows = lax.broadcasted_iota(jnp.int32, (t, LANES), 0)
            cols = lax.broadcasted_iota(jnp.int32, (t, LANES), 1)
            tiles = [jnp.where(cols + c * LANES <= rows, tl, -1e30) for c, tl in enumerate(tiles)]
        mx = tiles[0]
        for c in range(1, nt):
            mx = jnp.maximum(mx, tiles[c])
        m_prev = m_sc[...]
        m_new = jnp.maximum(m_prev, jnp.max(mx, axis=1, keepdims=True))
        alpha = jnp.exp2(m_prev - m_new)
        p = jnp.concatenate([jnp.exp2(tl - m_new).astype(BF16) for tl in tiles], axis=1)
        pv = jnp.dot(p, vb, preferred_element_type=F32)
        acc_sc[...] = jnp.concatenate([alpha, alpha], axis=1) * acc_sc[...] + pv
        m_sc[...] = m_new

    def stage(j, s_cur, s_next):
        scores(j + 1, s_next)
        softmax_pv(j, s_cur, False)

    scores(0, s0_sc)

    def pair(jp, carry):
        stage(2 * jp, s0_sc, s1_sc)
        stage(2 * jp + 1, s1_sc, s0_sc)
        return carry

    lax.fori_loop(0, i // 2, pair, 0)

    @pl.when(i % 2 == 1)
    def _():
        stage(i - 1, s0_sc, s1_sc)
        softmax_pv(i, s1_sc, True)

    @pl.when(i % 2 == 0)
    def _():
        softmax_pv(i, s0_sc, True)

    o_ref[...] = (acc_sc[:, :LANES] / acc_sc[:, LANES:]).astype(o_ref.dtype)


def flash_attention(q_nope, q_pe, kv, k_pe, nh):
    s = q_nope.shape[0]
    dh = LANES
    tq = _pick(s, (FLASH_T, 512, 256, 128))
    vm = 2 * (3 * _nbytes((s, dh), BF16) + 3 * _nbytes((tq, dh), BF16))
    vm += 5 * _nbytes((tq, tq), F32) + 4 * _nbytes((tq, 2 * dh), F32)
    scratch = [pltpu.VMEM((tq, 2 * dh), BF16), pltpu.VMEM((tq, dh), F32), pltpu.VMEM((tq, 2 * dh), F32),
               pltpu.VMEM((tq, tq), F32), pltpu.VMEM((tq, tq), F32)]
    return pl.pallas_call(
        functools.partial(_flash_kernel, t=tq),
        out_shape=jax.ShapeDtypeStruct((s, nh * dh), BF16),
        grid=(nh, s // tq),
        in_specs=[pl.BlockSpec((tq, dh), lambda h, i: (i, h)),
                  pl.BlockSpec((tq, dh), lambda h, i: (i, h)),
                  pl.BlockSpec((s, dh), lambda h, i: (0, h)),
                  pl.BlockSpec((s, dh), lambda h, i: (0, 0)),
                  pl.BlockSpec((s, dh), lambda h, i: (0, nh + h))],
        out_specs=pl.BlockSpec((tq, dh), lambda h, i: (i, h)),
        scratch_shapes=scratch,
        compiler_params=_params(("parallel", "arbitrary"), vm),
        name="mla_flash_attention",
    )(q_nope, q_pe, kv, k_pe, kv)


def _spatial_kernel(w_ref, bias_ref, v_ref, u_ref, o_ref, *, nch):
    c = CHUNK
    row = lax.broadcasted_iota(jnp.int32, (c, c), 0)
    col = lax.broadcasted_iota(jnp.int32, (c, c), 1)
    w = jnp.where(col <= row, w_ref[0], 0.0).astype(BF16)
    bias = bias_ref[0]
    for ci in range(nch):
        sl = pl.ds(ci * c, c)
        sg = jnp.dot(w, v_ref[sl, :], preferred_element_type=F32) + bias
        o_ref[sl, :] = (u_ref[sl, :].astype(F32) * sg).astype(o_ref.dtype)


def spatial_gate(z, vn, w_sp, b_sp):
    s, d = vn.shape
    g = w_sp.shape[0]
    gd = d // g
    nch = _pick(s // CHUNK, (16, 8, 4, 2, 1))
    rows = nch * CHUNK
    bias = jnp.broadcast_to(b_sp.astype(F32)[:, :, None], (g, CHUNK, gd))
    return pl.pallas_call(
        functools.partial(_spatial_kernel, nch=nch),
        out_shape=jax.ShapeDtypeStruct((s, d), BF16),
        grid=(g, s // rows),
        in_specs=[pl.BlockSpec((1, CHUNK, CHUNK), lambda gi, ci: (gi, 0, 0)),
                  pl.BlockSpec((1, CHUNK, gd), lambda gi, ci: (gi, 0, 0)),
                  pl.BlockSpec((rows, gd), lambda gi, ci: (ci, gi)),
                  pl.BlockSpec((rows, gd), lambda gi, ci: (ci, gi))],
        out_specs=pl.BlockSpec((rows, gd), lambda gi, ci: (ci, gi)),
        compiler_params=_params(("parallel", "parallel"), 16 << 20),
        name="spatial_gate",
    )(w_sp.astype(F32), bias, vn, z)


def _norm_router_kernel(h_ref, g_ref, wr_ref, xn_ref, lg_ref, *, eps):
    x = h_ref[...]
    ms = jnp.mean(x * x, axis=-1, keepdims=True)
    xn = x * lax.rsqrt(ms + eps) * g_ref[...]
    x_hi = xn.astype(BF16)
    xn_ref[...] = x_hi
    x_lo = (xn - x_hi.astype(F32)).astype(BF16)
    w = wr_ref[...]
    w_hi = w.astype(BF16)
    w_lo = (w - w_hi.astype(F32)).astype(BF16)
    lg_ref[...] = (jnp.dot(x_hi, w_hi, preferred_element_type=F32) + jnp.dot(x_hi, w_lo, preferred_element_type=F32)
                   + jnp.dot(x_lo, w_hi, preferred_element_type=F32))


def norm_router(h, g, w_router):
    m, d = h.shape
    e = w_router.shape[1]
    wr = jnp.pad(w_router.astype(F32), ((0, 0), (0, LANES - e)))
    tm = _pick(m, (512, 256, 128, 64, 32, 16, 8))
    vm = 2 * (_nbytes((tm, d), F32) + _nbytes((tm, d), BF16) + _nbytes((d, LANES), F32)) + 6 * _nbytes((tm, d), F32)
    xn, lg = pl.pallas_call(
        functools.partial(_norm_router_kernel, eps=RMS_EPS),
        out_shape=(jax.ShapeDtypeStruct((m, d), BF16), jax.ShapeDtypeStruct((m, LANES), F32)),
        grid=(m // tm,),
        in_specs=[pl.BlockSpec((tm, d), lambda i: (i, 0)),
                  pl.BlockSpec((1, d), lambda i: (0, 0)),
                  pl.BlockSpec((d, LANES), lambda i: (0, 0))],
        out_specs=(pl.BlockSpec((tm, d), lambda i: (i, 0)), pl.BlockSpec((tm, LANES), lambda i: (i, 0))),
        compiler_params=_params(("parallel",), vm),
        name="norm_router",
    )(h, g.reshape(1, d).astype(F32), wr)
    return xn, lg[:, :e]


def _make_epi_moe(tn, f_expert):
    def epi(accs, extras):
        g, u = accs
        gates = extras[0]
        e = (pl.program_id(1) * tn) // f_expert
        lane = lax.broadcasted_iota(jnp.int32, gates.shape, 1)
        gate = jnp.sum(jnp.where(lane == e, gates, 0.0), axis=1, keepdims=True)
        return gate * (g * jax.nn.sigmoid(g) * u)
    return epi


def _rw_prep_kernel(yr_ref, yk_ref, yv_ref, yl_ref, pr_ref, pk_ref, pv_ref, pl_ref,
                    mur_ref, muk_ref, muv_ref, mul_ref, w2_ref, a2_ref, g2_ref, w0_ref, a0_ref,
                    r_ref, k_ref, v_ref, lw_ref, ai_ref, g_ref):
    first = pl.program_id(0) == 0

    def shifted(x_ref, p_ref, mu_ref):
        x = x_ref[...].astype(F32)
        prev = jnp.where(first, 0.0, p_ref[...].astype(F32)[7:8, :])
        row = lax.broadcasted_iota(jnp.int32, x.shape, 0)
        xp = jnp.where(row == 0, prev, pltpu.roll(x, 1, axis=0))
        return x + (xp - x) * mu_ref[...]

    r_ref[...] = shifted(yr_ref, pr_ref, mur_ref)
    k_ref[...] = shifted(yk_ref, pk_ref, muk_ref)
    v_ref[...] = shifted(yv_ref, pv_ref, muv_ref)
    lo = shifted(yl_ref, pl_ref, mul_ref)
    lane = lax.broadcasted_iota(jnp.int32, lo.shape, 1)
    act = jnp.where(lane < DECAY_LORA, jnp.tanh(lo),
                    jnp.where(lane < DECAY_LORA + AAA_LORA, lo,
                              jnp.where(lane < DECAY_LORA + AAA_LORA + GATE_LORA, jax.nn.sigmoid(lo), 0.0)))
    act = act.astype(BF16)
    w_pre = w0_ref[...] + jnp.dot(act, w2_ref[...], preferred_element_type=F32)
    x = -w_pre
    softplus = jnp.maximum(x, 0.0) + jnp.log(1.0 + jnp.exp(-jnp.abs(x)))
    lw_ref[...] = -jnp.exp(-softplus - 0.5)
    ai_ref[...] = jax.nn.sigmoid(a0_ref[...] + jnp.dot(act, a2_ref[...], preferred_element_type=F32))
    g_ref[...] = jnp.dot(act, g2_ref[...], preferred_element_type=F32)


def _rwkv_group(y, col_lo, mu, w0, w2, a0, a2, g2, k_k, k_a, r_k, lnx_w, lnx_b):
    s = y.shape[0]
    dim = RW_DIM
    lora = DECAY_LORA + AAA_LORA + GATE_LORA
    lwid = -(-lora // LANES) * LANES
    assert col_lo % lwid == 0 and dim % LANES == 0
    tm = _pick(s, (128, 64, 32, 16, 8))
    pb = tm // 8
    mu_rkv = mu[:3 * dim].reshape(1, 3 * dim)
    mu_lo = jnp.pad(mu[3 * dim:], (0, lwid - lora)).reshape(1, lwid)

    def pad_rows(w, start):
        return jnp.pad(w, ((start, lwid - start - w.shape[0]), (0, 0))).astype(BF16)

    y_spec = lambda c: pl.BlockSpec((tm, dim), lambda i: (i, c))
    p_spec = lambda c: pl.BlockSpec((8, dim), lambda i: (jnp.maximum(i * pb - 1, 0), c))
    mu_spec = lambda c: pl.BlockSpec((1, dim), lambda i: (0, c))
    cl = col_lo // lwid
    w_spec = pl.BlockSpec((lwid, dim), lambda i: (0, 0))
    row_spec = pl.BlockSpec((1, dim), lambda i: (0, 0))
    out_spec = pl.BlockSpec((tm, dim), lambda i: (i, 0))
    out_sds = jax.ShapeDtypeStruct((s, dim), F32)
    vm = 2 * (3 * _nbytes((tm, dim), BF16) + 6 * _nbytes((tm, dim), F32) + 3 * _nbytes((lwid, dim), BF16))
    vm += 6 * _nbytes((tm, dim), F32)
    r, k, v, lw, ai, g = pl.pallas_call(
        _rw_prep_kernel,
        out_shape=(out_sds,) * 6,
        grid=(s // tm,),
        in_specs=[y_spec(0), y_spec(1), y_spec(2), pl.BlockSpec((tm, lwid), lambda i: (i, cl)),
                  p_spec(0), p_spec(1), p_spec(2),
                  pl.BlockSpec((8, lwid), lambda i: (jnp.maximum(i * pb - 1, 0), cl)),
                  mu_spec(0), mu_spec(1), mu_spec(2), pl.BlockSpec((1, lwid), lambda i: (0, 0)),
                  w_spec, w_spec, w_spec, row_spec, row_spec],
        out_specs=(out_spec,) * 6,
        compiler_params=_params(("parallel",), vm),
        name="rwkv_prep",
    )(y, y, y, y, y, y, y, y, mu_rkv, mu_rkv, mu_rkv, mu_lo,
      pad_rows(w2, 0), pad_rows(a2, DECAY_LORA), pad_rows(g2, DECAY_LORA + AAA_LORA),
      w0.reshape(1, dim), a0.reshape(1, dim))
    return rwkv_mix(r, k, v, ai, lw, g, k_k, k_a, r_k.reshape(-1), lnx_w, lnx_b)


def _rope(x, cos, sin):
    half = x.shape[-1] // 2
    x1, x2 = x[..., :half], x[..., half:]
    return jnp.concatenate([x1 * cos - x2 * sin, x2 * cos + x1 * sin], axis=-1)


def _mla_group(y, col_q, col_kv, col_pe, cos, sin, q_norm, w_uq, kv_norm, w_ukv):
    s = y.shape[0]
    nh = MLA_HEADS
    cqn = rmsnorm(y, q_norm, BF16, width=Q_LORA, col_block=col_q // Q_LORA)
    ckvn = rmsnorm(y, kv_norm, BF16, width=KV_LORA, col_block=col_kv // KV_LORA)
    assert QK_NOPE == LANES and V_HEAD == LANES and QK_ROPE <= LANES
    half = QK_ROPE // 2
    scale = (QK_NOPE + QK_ROPE) ** -0.5 * 1.4426950408889634
    wq = w_uq.reshape(Q_LORA, nh, QK_NOPE + QK_ROPE)
    w_nope = wq[:, :, :QK_NOPE].reshape(Q_LORA, nh * LANES).astype(BF16)
    w_pe = wq[:, :, QK_NOPE:]
    w_rot = jnp.concatenate([-w_pe[..., half:], w_pe[..., :half]], axis=-1)
    lane_pad = ((0, 0), (0, 0), (0, LANES - QK_ROPE))
    w_pe = jnp.pad(w_pe, lane_pad).reshape(Q_LORA, nh * LANES).astype(BF16)
    w_rot = jnp.pad(w_rot, lane_pad).reshape(Q_LORA, nh * LANES).astype(BF16)
    tm = _pick(s, (1024, 512, 256, 128, 64, 32, 16, 8))
    tn = _pick(nh * LANES, (512, 256, 128))
    table_pad = ((0, 0), (0, LANES - QK_ROPE))
    cos_t = jnp.tile(jnp.pad(jnp.concatenate([cos, cos], axis=-1), table_pad), (1, tn // LANES))
    sin_t = jnp.tile(jnp.pad(jnp.concatenate([sin, sin], axis=-1), table_pad), (1, tn // LANES))
    q_nope = matmul(cqn, [w_nope], lambda accs, extras: accs[0] * scale, BF16, tm=tm, tn=tn, name="mla_q_nope_up")
    q_pe = matmul(cqn, [w_pe, w_rot], lambda accs, extras: (accs[0] * extras[0] + accs[1] * extras[1]) * scale,
                  BF16, tm=tm, tn=tn,
                  extras=[(cos_t, (tm, tn), lambda i, jn: (i, 0)), (sin_t, (tm, tn), lambda i, jn: (i, 0))],
                  name="mla_q_rope_up")
    w_kv = jnp.transpose(w_ukv.reshape(KV_LORA, nh, 2, LANES), (0, 2, 1, 3)).reshape(KV_LORA, 2 * nh * LANES)
    kv = matmul(ckvn, [w_kv.astype(BF16)], _epi_id, BF16, tm=tm, tn=tn, name="mla_kv_up")
    k_pe = _rope(y[:, col_pe:col_pe + QK_ROPE].astype(F32), cos, sin)
    k_pe = jnp.pad(k_pe, table_pad).astype(BF16)
    return flash_attention(q_nope, q_pe, kv, k_pe, nh)


def _even_layer(h, hn, j, cos, sin, prm):
    w_in = prm["ev_in_proj"][j]
    rw_end = 3 * RW_DIM
    lora_w = DECAY_LORA + AAA_LORA + GATE_LORA
    rw_proj = rw_end + lora_w
    w_perm = jnp.concatenate([
        w_in[:, :rw_end],
        w_in[:, rw_proj:rw_proj + Q_LORA + KV_LORA],
        w_in[:, rw_end:rw_proj],
        w_in[:, rw_proj + Q_LORA + KV_LORA:],
    ], axis=1).astype(BF16)
    y = matmul(hn, [w_perm], _epi_id, BF16, tm=1024, tn=512, name="even_in_proj")
    col_q = rw_end
    col_kv = rw_end + Q_LORA
    col_lo = rw_end + Q_LORA + KV_LORA
    col_pe = col_lo + lora_w
    mu = prm["rw_mu"][j]
    o_rw = _rwkv_group(y, col_lo, mu, prm["rw_w0"][j], prm["rw_w2"][j], prm["rw_a0"][j], prm["rw_a2"][j],
                       prm["rw_g2"][j], prm["rw_k_k"][j], prm["rw_k_a"][j], prm["rw_r_k"][j],
                       prm["rw_lnx_w"][j], prm["rw_lnx_b"][j])
    o_mla = _mla_group(y, col_q, col_kv, col_pe, cos, sin, prm["mla_q_norm"][j], prm["mla_w_uq"][j],
                       prm["mla_kv_norm"][j], prm["mla_w_ukv"][j])
    mix = jnp.concatenate([o_rw, o_mla], axis=-1)
    tm, tn = 1024, 512
    h = matmul(mix, [prm["ev_out_proj"][j].astype(BF16)], _epi_residual, F32, tm=tm, tn=tn,
               extras=[_res_extra(h, _pick(h.shape[0], (tm, 512, 256, 128, 64, 32, 16, 8)),
                                  _pick(h.shape[1], (tn, 512, 256, 128)), 1)], name="even_out_proj")
    return h


def _dense_ffn(h, hn, wg, wu, wd):
    d, f = wg.shape
    fp = -(-f // 1024) * 1024 if f > 1024 else f
    pad = fp - f
    wg = jnp.pad(wg.astype(BF16), ((0, 0), (0, pad)))
    wu = jnp.pad(wu.astype(BF16), ((0, 0), (0, pad)))
    wd = jnp.pad(wd.astype(BF16), ((0, pad), (0, 0)))
    mid = matmul(hn, [wg, wu], _epi_swiglu, BF16, tm=1024, tn=512, name="ffn_up")
    tm = _pick(h.shape[0], (1024, 512, 256, 128, 64, 32, 16, 8))
    tn = _pick(h.shape[1], (1024, 512, 256, 128))
    tk = _pick(fp, (2816, 2048, 1024, 512, 256, 128))
    nk = fp // tk
    return matmul(mid, [wd], _epi_residual, F32, tm=tm, tn=tn, tk=tk,
                  extras=[_res_extra(h, tm, tn, nk)], name="ffn_down")


def _moe(h, g_norm, w_router, wg, wu, wd):
    ne, d, f = wg.shape
    xn, logits = norm_router(h, g_norm, w_router)
    top_val, top_idx = lax.top_k(logits, TOP_K)
    top_w = jax.nn.softmax(top_val, axis=-1)
    gates = jnp.sum(jax.nn.one_hot(top_idx, ne, dtype=F32) * top_w[..., None], axis=-2)
    m = h.shape[0]
    tm = _pick(m, (1024, 512, 256, 128, 64, 32, 16, 8))
    tn = _pick(f, (512, 256, 128))
    nt = f // tn
    w_spec = pl.BlockSpec((None, d, tn), lambda i, jn: (jn // nt, 0, jn % nt))
    mid = matmul(xn, [wg.astype(BF16), wu.astype(BF16)], _make_epi_moe(tn, f), BF16, tm=tm, tn=tn,
                 w_specs=[w_spec, w_spec], n_out=ne * f,
                 extras=[(gates, (tm, ne), lambda i, jn: (i, 0))], name="moe_up")
    wd2 = wd.astype(BF16).reshape(ne * f, d)
    tn2 = _pick(d, (1024, 512, 256, 128))
    tk = _pick(ne * f, (3072, 2048, 1024, 512, 256, 128))
    nk = ne * f // tk
    return matmul(mid, [wd2], _epi_residual, F32, tm=tm, tn=tn2, tk=tk,
                  extras=[_res_extra(h, tm, tn2, nk)], name="moe_down")


def _odd_layer(h, hn, j, prm):
    z = matmul(hn, [prm["sg_in_proj"][j].astype(BF16)], _epi_gelu, BF16, tm=1024, tn=512, name="sg_in_proj")
    d = h.shape[1]
    vn = layernorm(z, prm["sg_ln_w"][j], prm["sg_ln_b"][j], BF16, width=d, col_block=1)
    gated = spatial_gate(z, vn, prm["sg_w_spatial"][j], prm["sg_b_spatial"][j])
    tm = _pick(h.shape[0], (1024, 512, 256, 128, 64, 32, 16, 8))
    tn = _pick(d, (512, 256, 128))
    return matmul(gated, [prm["sg_out_proj"][j].astype(BF16)], _epi_residual, F32, tm=tm, tn=tn,
                  extras=[_res_extra(h, tm, tn, 1)], name="sg_out_proj")


def kernel(x, p, positions, norm_mix, norm_ffn, norm_ple, final_norm, ev_in_proj, rw_mu, rw_w0, rw_w2, rw_a0, rw_a2, rw_g2, rw_k_k, rw_k_a, rw_r_k, rw_lnx_w, rw_lnx_b, mla_q_norm, mla_w_uq, mla_kv_norm, mla_w_ukv, ev_out_proj, ffn_w_gate, ffn_w_up, ffn_w_down, sg_in_proj, sg_ln_w, sg_ln_b, sg_w_spatial, sg_b_spatial, sg_out_proj, moe_router, moe_w_gate, moe_w_up, moe_w_down, ple_proj, ple_gate):
    prm = dict(ev_in_proj=ev_in_proj, rw_mu=rw_mu, rw_w0=rw_w0, rw_w2=rw_w2, rw_a0=rw_a0, rw_a2=rw_a2, rw_g2=rw_g2,
               rw_k_k=rw_k_k, rw_k_a=rw_k_a, rw_r_k=rw_r_k, rw_lnx_w=rw_lnx_w, rw_lnx_b=rw_lnx_b,
               mla_q_norm=mla_q_norm, mla_w_uq=mla_w_uq, mla_kv_norm=mla_kv_norm, mla_w_ukv=mla_w_ukv,
               ev_out_proj=ev_out_proj, sg_in_proj=sg_in_proj, sg_ln_w=sg_ln_w, sg_ln_b=sg_ln_b,
               sg_w_spatial=sg_w_spatial, sg_b_spatial=sg_b_spatial, sg_out_proj=sg_out_proj)
    bsz, s, d = x.shape
    assert bsz == 1
    inv = 1.0 / (ROPE_THETA ** (jnp.arange(0, QK_ROPE, 2, dtype=F32) / QK_ROPE))
    ang = positions[0].astype(F32)[:, None] * inv
    cos, sin = jnp.cos(ang), jnp.sin(ang)
    h = x[0]
    for i in range(DEPTH):
        j = i // 2
        if i % 2 == 0:
            hn = rmsnorm(h, norm_mix[i], BF16)
            h = _even_layer(h, hn, j, cos, sin, prm)
            hn = rmsnorm(h, norm_ffn[i], BF16)
            h = _dense_ffn(h, hn, ffn_w_gate[j], ffn_w_up[j], ffn_w_down[j])
        else:
            hn = rmsnorm(h, norm_mix[i], BF16)
            h = _odd_layer(h, hn, j, prm)
            h = _moe(h, norm_ffn[i], moe_router[j], moe_w_gate[j], moe_w_up[j], moe_w_down[j])
        hn = rmsnorm(h, norm_ple[i], BF16)
        h = ple(hn, ple_gate[i].astype(BF16), p[i, 0].astype(BF16), ple_proj[i].astype(BF16), h)
    return rmsnorm(h, final_norm, F32)[None]
```

```python
import functools

import jax
import jax.numpy as jnp
from jax import lax
from jax.experimental import pallas as pl
from jax.experimental.pallas import tpu as pltpu

D_MODEL = 4096
DEPTH = 4
RW_HEADS = 32
RW_HEAD = 64
RW_DIM = RW_HEADS * RW_HEAD
DECAY_LORA = 96
AAA_LORA = 96
GATE_LORA = 256
GN_EPS = 64e-5
MLA_HEADS = 16
Q_LORA = 1536
KV_LORA = 512
QK_NOPE = 128
QK_ROPE = 64
V_HEAD = 128
ROPE_THETA = 10000.0
CHUNK = 128
SG_GROUPS = 32
LN_EPS = 1e-5
D_FF = 11008
N_EXPERTS = 8
TOP_K = 2
D_FF_EXPERT = 1536
PLE_DIM = 256
RMS_EPS = 1e-6

V7X_VMEM_BYTES = 64 * 1024 * 1024
LANES = 128
RW_CHUNK = 64
RW_HEADS_PER_STEP = 8
RW_CHUNKS_PER_STEP = 2
FLASH_T = 1024
MOE_TILE = 512

F32 = jnp.float32
BF16 = jnp.bfloat16


def _pick(dim, candidates):
    for c in candidates:
        if dim % c == 0:
            return c
    return dim


def _params(sem, vmem_bytes, flags=None):
    limit = int(min(V7X_VMEM_BYTES - 6 * 1024 * 1024, max(vmem_bytes * 5 // 4 + (2 << 20), 16 << 20)))
    return pltpu.CompilerParams(dimension_semantics=sem, vmem_limit_bytes=limit, flags=flags)


def _nbytes(shape, dtype):
    n = 1
    for s in shape:
        n *= s
    return n * jnp.dtype(dtype).itemsize


def _rmsnorm_kernel(x_ref, g_ref, o_ref, *, eps):
    x = x_ref[...].astype(F32)
    ms = jnp.mean(x * x, axis=-1, keepdims=True)
    o_ref[...] = (x * lax.rsqrt(ms + eps) * g_ref[...]).astype(o_ref.dtype)


def rmsnorm(x, g, out_dtype, *, width=None, col_block=0):
    m = x.shape[0]
    d = x.shape[1] if width is None else width
    tm = _pick(m, (512, 256, 128, 64, 32, 16, 8))
    vm = 2 * (_nbytes((tm, d), x.dtype) + _nbytes((tm, d), out_dtype)) + 4 * _nbytes((tm, d), F32)
    return pl.pallas_call(
        functools.partial(_rmsnorm_kernel, eps=RMS_EPS),
        out_shape=jax.ShapeDtypeStruct((m, d), out_dtype),
        grid=(m // tm,),
        in_specs=[pl.BlockSpec((tm, d), lambda i: (i, col_block)),
                  pl.BlockSpec((1, d), lambda i: (0, 0))],
        out_specs=pl.BlockSpec((tm, d), lambda i: (i, 0)),
        compiler_params=_params(("parallel",), vm),
        name="rmsnorm",
    )(x, g.reshape(1, d).astype(F32))


def _layernorm_kernel(x_ref, w_ref, b_ref, o_ref, *, eps):
    x = x_ref[...].astype(F32)
    mean = jnp.mean(x, axis=-1, keepdims=True)
    var = jnp.mean(jnp.square(x - mean), axis=-1, keepdims=True)
    o_ref[...] = ((x - mean) * lax.rsqrt(var + eps) * w_ref[...] + b_ref[...]).astype(o_ref.dtype)


def layernorm(x, w, b, out_dtype, *, width, col_block):
    m = x.shape[0]
    tm = _pick(m, (512, 256, 128, 64, 32, 16, 8))
    vm = 2 * (_nbytes((tm, width), x.dtype) + _nbytes((tm, width), out_dtype)) + 4 * _nbytes((tm, width), F32)
    return pl.pallas_call(
        functools.partial(_layernorm_kernel, eps=LN_EPS),
        out_shape=jax.ShapeDtypeStruct((m, width), out_dtype),
        grid=(m // tm,),
        in_specs=[pl.BlockSpec((tm, width), lambda i: (i, col_block)),
                  pl.BlockSpec((1, width), lambda i: (0, 0)),
                  pl.BlockSpec((1, width), lambda i: (0, 0))],
        out_specs=pl.BlockSpec((tm, width), lambda i: (i, 0)),
        compiler_params=_params(("parallel",), vm),
        name="layernorm",
    )(x, w.reshape(1, width).astype(F32), b.reshape(1, width).astype(F32))


def _mm_kernel(*refs, nk, n_w, n_extra, epi):
    a_ref = refs[0]
    w_refs = refs[1:1 + n_w]
    extra_refs = refs[1 + n_w:1 + n_w + n_extra]
    o_ref = refs[1 + n_w + n_extra]
    acc_refs = refs[2 + n_w + n_extra:]
    a = a_ref[...]
    if nk == 1:
        accs = [jnp.dot(a, w[...], preferred_element_type=F32) for w in w_refs]
        o_ref[...] = epi(accs, [e[...] for e in extra_refs]).astype(o_ref.dtype)
        return
    k = pl.program_id(2)

    @pl.when(k == 0)
    def _():
        for acc in acc_refs:
            acc[...] = jnp.zeros_like(acc)

    for acc, w in zip(acc_refs, w_refs):
        acc[...] += jnp.dot(a, w[...], preferred_element_type=F32)

    @pl.when(k == nk - 1)
    def _():
        o_ref[...] = epi([acc[...] for acc in acc_refs], [e[...] for e in extra_refs]).astype(o_ref.dtype)


def matmul(a, ws, epi, out_dtype, *, extras=(), tm=1024, tn=512, tk=None, a_col_block=0, a_width=None,
           w_specs=None, n_out=None, name="matmul"):
    m = a.shape[0]
    kdim = a.shape[1] if a_width is None else a_width
    n = ws[0].shape[-1] if n_out is None else n_out
    tm = _pick(m, (tm, 512, 256, 128, 64, 32, 16, 8))
    tn = _pick(n, (tn, 512, 256, 128))
    tk = kdim if tk is None else _pick(kdim, (tk,))
    nk = kdim // tk
    kb0 = a_col_block * nk
    if nk == 1:
        grid = (m // tm, n // tn)
        a_spec = pl.BlockSpec((tm, tk), lambda i, j: (i, kb0))
        w_spec_list = w_specs or [pl.BlockSpec((tk, tn), lambda i, j: (0, j)) for _ in ws]
        o_spec = pl.BlockSpec((tm, tn), lambda i, j: (i, j))
        sem = ("parallel", "parallel")
        scratch = []
    else:
        grid = (m // tm, n // tn, nk)
        a_spec = pl.BlockSpec((tm, tk), lambda i, j, k: (i, kb0 + k))
        w_spec_list = w_specs or [pl.BlockSpec((tk, tn), lambda i, j, k: (k, j)) for _ in ws]
        o_spec = pl.BlockSpec((tm, tn), lambda i, j, k: (i, j))
        sem = ("parallel", "parallel", "arbitrary")
        scratch = [pltpu.VMEM((tm, tn), F32) for _ in ws]
    extra_specs = [pl.BlockSpec(bs, im) for (_, bs, im) in extras]
    vm = 2 * _nbytes((tm, tk), a.dtype) + 2 * len(ws) * _nbytes((tk, tn), ws[0].dtype)
    vm += 2 * _nbytes((tm, tn), out_dtype) + (len(ws) + 2) * _nbytes((tm, tn), F32)
    for (arr, bs, _) in extras:
        vm += 2 * _nbytes([1 if b is None else b for b in bs], arr.dtype)
    return pl.pallas_call(
        functools.partial(_mm_kernel, nk=nk, n_w=len(ws), n_extra=len(extras), epi=epi),
        out_shape=jax.ShapeDtypeStruct((m, n), out_dtype),
        grid=grid,
        in_specs=[a_spec] + list(w_spec_list) + extra_specs,
        out_specs=o_spec,
        scratch_shapes=scratch,
        compiler_params=_params(sem, vm),
        name=name,
    )(a, *ws, *[e[0] for e in extras])


def _epi_id(accs, extras):
    return accs[0]


def _epi_residual(accs, extras):
    return extras[0] + accs[0]


def _epi_swiglu(accs, extras):
    g, u = accs
    return g * jax.nn.sigmoid(g) * u


def _epi_gelu(accs, extras):
    z = accs[0]
    return 0.5 * z * (1.0 + lax.erf(z * (2.0 ** -0.5)))


def _res_extra(h, tm, tn, nk):
    if nk == 1:
        return (h, (tm, tn), lambda i, j: (i, j))
    return (h, (tm, tn), lambda i, j, k: (i, j))


def _ple_kernel(hn_ref, wg_ref, p_ref, wp_ref, h_ref, o_ref):
    gate = jax.nn.sigmoid(jnp.dot(hn_ref[...], wg_ref[...], preferred_element_type=F32))
    proj = jnp.dot(p_ref[...], wp_ref[...], preferred_element_type=F32)
    o_ref[...] = h_ref[...] + gate * proj


def ple(hn, w_gate, p, w_proj, h):
    m, d = hn.shape
    n = w_gate.shape[1]
    pd = p.shape[1]
    tm = _pick(m, (1024, 512, 256, 128, 64, 32, 16, 8))
    tn = _pick(n, (512, 256, 128))
    vm = 2 * (_nbytes((tm, d), BF16) + _nbytes((d, tn), BF16) + _nbytes((tm, pd), BF16) + _nbytes((pd, tn), BF16))
    vm += 6 * _nbytes((tm, tn), F32)
    return pl.pallas_call(
        _ple_kernel,
        out_shape=jax.ShapeDtypeStruct((m, n), F32),
        grid=(m // tm, n // tn),
        in_specs=[pl.BlockSpec((tm, d), lambda i, j: (i, 0)),
                  pl.BlockSpec((d, tn), lambda i, j: (0, j)),
                  pl.BlockSpec((tm, pd), lambda i, j: (i, 0)),
                  pl.BlockSpec((pd, tn), lambda i, j: (0, j)),
                  pl.BlockSpec((tm, tn), lambda i, j: (i, j))],
        out_specs=pl.BlockSpec((tm, tn), lambda i, j: (i, j)),
        compiler_params=_params(("parallel", "parallel"), vm),
        name="ple",
    )(hn, w_gate, p, w_proj, h)


def _bdot(a, b):
    return jnp.dot(a.astype(BF16), b.astype(BF16), preferred_element_type=F32)


def _bdot_nt(a, b):
    return lax.dot_general(a.astype(BF16), b.astype(BF16), (((1,), (1,)), ((), ())),
                           preferred_element_type=F32)


def _bdot_tn(a, b):
    return lax.dot_general(a.astype(BF16), b.astype(BF16), (((0,), (0,)), ((), ())),
                           preferred_element_type=F32)


def _split3(x):
    hi = x.astype(BF16)
    r1 = x - hi.astype(F32)
    mid = r1.astype(BF16)
    lo = (r1 - mid.astype(F32)).astype(BF16)
    return hi, mid, lo


def _bmm(a, b):
    return jnp.einsum("bij,bjk->bik", a.astype(BF16), b.astype(BF16), preferred_element_type=F32)


def _bmm_nt(a, b):
    return jnp.einsum("bik,bjk->bij", a.astype(BF16), b.astype(BF16), preferred_element_type=F32)


def _bmm_tn(a, b):
    return jnp.einsum("bti,btj->bij", a.astype(BF16), b.astype(BF16), preferred_element_type=F32)


def _rwkv_a_kernel(r_ref, k_ref, v_ref, ai_ref, lw_ref, kk_ref, ka_ref, mt_ref, nt_ref, q_ref, o0_ref, *, hb, ncb):
    t = RW_CHUNK
    n = RW_HEAD
    nb = hb * ncb
    row = lax.broadcasted_iota(jnp.int32, (t, t), 0)
    col = lax.broadcasted_iota(jnp.int32, (t, t), 1)
    incl = col <= row
    strict = col < row
    tri = jnp.broadcast_to(incl.astype(BF16), (nb, t, t))
    eye = (row == col).astype(F32)
    drow = lax.broadcasted_iota(jnp.int32, (n, n), 0)
    dcol = lax.broadcasted_iota(jnp.int32, (n, n), 1)

    def units(ref):
        return jnp.stack([ref[c * t:(c + 1) * t, h * n:(h + 1) * n] for c in range(ncb) for h in range(hb)])

    def unit_params(ref):
        return jnp.stack([ref[:, h * n:(h + 1) * n] for c in range(ncb) for h in range(hb)])

    k = units(k_ref)
    ai = units(ai_ref)
    kk = k * unit_params(kk_ref)
    kk = kk * lax.rsqrt(jnp.maximum(jnp.sum(kk * kk, axis=-1, keepdims=True), 1e-24))
    k = k * (1.0 + (ai - 1.0) * unit_params(ka_ref))
    a = -kk
    b = kk * ai
    lw = units(lw_ref)
    hi, mid, lo = _split3(lw)
    cum = (jnp.einsum("bij,bjk->bik", tri, hi, preferred_element_type=F32)
           + jnp.einsum("bij,bjk->bik", tri, mid, preferred_element_type=F32)
           + jnp.einsum("bij,bjk->bik", tri, lo, preferred_element_type=F32))
    cum_t = cum[:, t - 1:t, :]
    e_in = jnp.exp(cum)
    e_out = jnp.exp(-cum)
    e_prev = jnp.exp(cum - lw)
    e_rem = jnp.exp(cum_t - cum)
    p_t = jnp.exp(cum_t)
    r = units(r_ref)
    v = units(v_ref)
    rh = r * e_in
    ah = a * e_prev
    bh = b * e_out
    kh = k * e_out
    bt = b * e_rem
    kt = k * e_rem
    n_ab = jnp.where(strict, _bmm_nt(ah, bh), 0.0)
    n_ak = jnp.where(strict, _bmm_nt(ah, kh), 0.0)
    l_rb = jnp.where(incl, _bmm_nt(rh, bh), 0.0)
    l_rk = jnp.where(incl, _bmm_nt(rh, kh), 0.0)
    x = eye + n_ab
    pw = n_ab
    for _ in range((t - 1).bit_length() - 1):
        pw = _bmm(pw, pw)
        x = x + _bmm(x, pw)
    akv = _bmm(n_ak, v)
    w = _bmm(x, ah)
    u0 = _bmm(x, akv)
    qt = rh + _bmm(l_rb, w)
    o0 = _bmm(l_rb, u0) + _bmm(l_rk, v)
    mt = jnp.where(drow == dcol, p_t, 0.0) + _bmm_tn(bt, w)
    nt = _bmm_tn(bt, u0) + _bmm_tn(kt, v)
    for c in range(ncb):
        rows = slice(c * t, (c + 1) * t)
        q_ref[rows, :] = jnp.concatenate([qt[c * hb + h] for h in range(hb)], axis=-1)
        o0_ref[rows, :] = jnp.concatenate([o0[c * hb + h] for h in range(hb)], axis=-1)
        mt_ref[:, c] = mt[c * hb:(c + 1) * hb]
        nt_ref[:, c] = nt[c * hb:(c + 1) * hb]


def _bdot3(a, b):
    a_hi = a.astype(BF16)
    a_lo = (a - a_hi.astype(F32)).astype(BF16)
    b_hi = b.astype(BF16)
    b_lo = (b - b_hi.astype(F32)).astype(BF16)
    mm = lambda x, y: jnp.einsum("bij,bjk->bik", x, y, preferred_element_type=F32)
    return mm(a_hi, b_hi) + mm(a_hi, b_lo) + mm(a_lo, b_hi)


def _rwkv_b_kernel(mt_ref, nt_ref, q_ref, o0_ref, r_ref, k_ref, v_ref, ai_ref, g_ref, ka_ref, rk_ref, lw_ref, lb_ref,
                   o_ref, z_ref, *, nh):
    n = RW_HEAD
    c = pl.program_id(0)

    @pl.when(c == 0)
    def _():
        z_ref[...] = jnp.zeros_like(z_ref)

    def heads(ref):
        return jnp.stack([ref[:, h * n:(h + 1) * n] for h in range(nh)])

    z = z_ref[...]
    o = heads(o0_ref) + _bdot3(heads(q_ref), z)
    z_ref[...] = nt_ref[:, 0] + _bdot3(mt_ref[:, 0], z)
    mean = jnp.mean(o, axis=-1, keepdims=True)
    var = jnp.mean(jnp.square(o - mean), axis=-1, keepdims=True)
    o = (o - mean) * lax.rsqrt(var + GN_EPS) * heads(lw_ref) + heads(lb_ref)
    k = heads(k_ref) * (1.0 + (heads(ai_ref) - 1.0) * heads(ka_ref))
    bonus = jnp.sum(heads(r_ref) * k * heads(rk_ref), axis=-1, keepdims=True) * heads(v_ref)
    out = (o + bonus) * heads(g_ref)
    o_ref[...] = jnp.concatenate([out[h] for h in range(nh)], axis=-1).astype(o_ref.dtype)


def rwkv_mix(r, k, v, ai, lw, g, k_k, k_a, r_k, lnx_w, lnx_b):
    s, dim = r.shape
    n = RW_HEAD
    nh = dim // n
    t = RW_CHUNK
    nc = s // t
    hb = _pick(nh, (RW_HEADS_PER_STEP, 4, 2))
    ncb = _pick(nc, (RW_CHUNKS_PER_STEP, 2, 1))
    row = lambda p: p.reshape(1, dim).astype(F32)
    seq_spec = pl.BlockSpec((ncb * t, hb * n), lambda g_, c: (c, g_))
    par_spec = pl.BlockSpec((1, hb * n), lambda g_, c: (0, g_))
    mat_spec = pl.BlockSpec((hb, ncb, n, n), lambda g_, c: (g_, c, 0, 0))
    mt, nt, qt, o0 = pl.pallas_call(
        functools.partial(_rwkv_a_kernel, hb=hb, ncb=ncb),
        out_shape=(jax.ShapeDtypeStruct((nh, nc, n, n), F32), jax.ShapeDtypeStruct((nh, nc, n, n), F32),
                   jax.ShapeDtypeStruct((s, dim), F32), jax.ShapeDtypeStruct((s, dim), F32)),
        grid=(nh // hb, nc // ncb),
        in_specs=[seq_spec] * 5 + [par_spec] * 2,
        out_specs=(mat_spec, mat_spec, seq_spec, seq_spec),
        compiler_params=_params(("parallel", "parallel"), 32 << 20),
        name="rwkv_chunk_local",
    )(r, k, v, ai, lw, row(k_k), row(k_a))
    seq_all = pl.BlockSpec((t, dim), lambda c: (c, 0))
    par_all = pl.BlockSpec((1, dim), lambda c: (0, 0))
    mat_all = pl.BlockSpec((nh, 1, n, n), lambda c: (0, c, 0, 0))
    return pl.pallas_call(
        functools.partial(_rwkv_b_kernel, nh=nh),
        out_shape=jax.ShapeDtypeStruct((s, dim), BF16),
        grid=(nc,),
        in_specs=[mat_all, mat_all] + [seq_all] * 7 + [par_all] * 4,
        out_specs=seq_all,
        scratch_shapes=[pltpu.VMEM((nh, n, n), F32)],
        compiler_params=_params(("arbitrary",), 32 << 20),
        name="rwkv_state_scan",
    )(mt, nt, qt, o0, r, k, v, ai, g, row(k_a), row(r_k), row(lnx_w), row(lnx_b))


def _flash_kernel(qn_ref, qp_ref, kn_ref, kp_ref, v_ref, o_ref, q_sc, m_sc, acc_sc, s0_sc, s1_sc, *, t):
    i = pl.program_id(1)
    nt = t // LANES
    q_sc[...] = jnp.concatenate([qn_ref[...], qp_ref[...]], axis=1)
    m_sc[...] = jnp.full_like(m_sc, -1e30)
    acc_sc[...] = jnp.zeros_like(acc_sc)
    ones = jnp.ones((t, LANES), BF16)

    def scores(j, s_ref):
        start = pl.multiple_of(j * t, t)
        kb = jnp.concatenate([kn_ref[pl.ds(start, t), :], kp_ref[pl.ds(start, t), :]], axis=1)
        s_ref[...] = lax.dot_general(q_sc[...], kb, (((1,), (1,)), ((), ())), preferred_element_type=F32)

    def softmax_pv(j, s_ref, masked):
        start = pl.multiple_of(j * t, t)
        vb = jnp.concatenate([v_ref[pl.ds(start, t), :], ones], axis=1)
        tiles = [s_ref[:, c * LANES:(c + 1) * LANES] for c in range(nt)]
        if masked:
            rows = lax.broadcasted_iota(jnp.int32, (t, LANES), 0)
            cols = lax.broadcasted_iota(jnp.int32, (t, LANES), 1)
            tiles = [jnp.where(cols + c * LANES <= rows, tl, -1e30) for c, tl in enumerate(tiles)]
        mx = tiles[0]
        for c in range(1, nt):
            mx = jnp.maximum(mx, tiles[c])
        m_prev = m_sc[...]
        m_new = jnp.maximum(m_prev, jnp.max(mx, axis=1, keepdims=True))
        alpha = jnp.exp2(m_prev - m_new)
        p = jnp.concatenate([jnp.exp2(tl - m_new).astype(BF16) for tl in tiles], axis=1)
        pv = jnp.dot(p, vb, preferred_element_type=F32)
        acc_sc[...] = jnp.concatenate([alpha, alpha], axis=1) * acc_sc[...] + pv
        m_sc[...] = m_new

    def stage(j, s_cur, s_next):
        scores(j + 1, s_next)
        softmax_pv(j, s_cur, False)

    scores(0, s0_sc)

    def pair(jp, carry):
        stage(2 * jp, s0_sc, s1_sc)
        stage(2 * jp + 1, s1_sc, s0_sc)
        return carry

    lax.fori_loop(0, i // 2, pair, 0)

    @pl.when(i % 2 == 1)
    def _():
        stage(i - 1, s0_sc, s1_sc)
        softmax_pv(i, s1_sc, True)

    @pl.when(i % 2 == 0)
    def _():
        softmax_pv(i, s0_sc, True)

    o_ref[...] = (acc_sc[:, :LANES] / acc_sc[:, LANES:]).astype(o_ref.dtype)


def flash_attention(q_nope, q_pe, kv, k_pe, nh):
    s = q_nope.shape[0]
    dh = LANES
    tq = _pick(s, (FLASH_T, 512, 256, 128))
    vm = 2 * (3 * _nbytes((s, dh), BF16) + 3 * _nbytes((tq, dh), BF16))
    vm += 5 * _nbytes((tq, tq), F32) + 4 * _nbytes((tq, 2 * dh), F32)
    scratch = [pltpu.VMEM((tq, 2 * dh), BF16), pltpu.VMEM((tq, dh), F32), pltpu.VMEM((tq, 2 * dh), F32),
               pltpu.VMEM((tq, tq), F32), pltpu.VMEM((tq, tq), F32)]
    return pl.pallas_call(
        functools.partial(_flash_kernel, t=tq),
        out_shape=jax.ShapeDtypeStruct((s, nh * dh), BF16),
        grid=(nh, s // tq),
        in_specs=[pl.BlockSpec((tq, dh), lambda h, i: (i, h)),
                  pl.BlockSpec((tq, dh), lambda h, i: (i, h)),
                  pl.BlockSpec((s, dh), lambda h, i: (0, h)),
                  pl.BlockSpec((s, dh), lambda h, i: (0, 0)),
                  pl.BlockSpec((s, dh), lambda h, i: (0, nh + h))],
        out_specs=pl.BlockSpec((tq, dh), lambda h, i: (i, h)),
        scratch_shapes=scratch,
        compiler_params=_params(("parallel", "arbitrary"), vm),
        name="mla_flash_attention",
    )(q_nope, q_pe, kv, k_pe, kv)


def _spatial_kernel(w_ref, bias_ref, v_ref, u_ref, o_ref, *, nch):
    c = CHUNK
    row = lax.broadcasted_iota(jnp.int32, (c, c), 0)
    col = lax.broadcasted_iota(jnp.int32, (c, c), 1)
    w = jnp.where(col <= row, w_ref[0], 0.0).astype(BF16)
    bias = bias_ref[0]
    for ci in range(nch):
        sl = pl.ds(ci * c, c)
        sg = jnp.dot(w, v_ref[sl, :], preferred_element_type=F32) + bias
        o_ref[sl, :] = (u_ref[sl, :].astype(F32) * sg).astype(o_ref.dtype)


def spatial_gate(z, vn, w_sp, b_sp):
    s, d = vn.shape
    g = w_sp.shape[0]
    gd = d // g
    nch = _pick(s // CHUNK, (16, 8, 4, 2, 1))
    rows = nch * CHUNK
    bias = jnp.broadcast_to(b_sp.astype(F32)[:, :, None], (g, CHUNK, gd))
    return pl.pallas_call(
        functools.partial(_spatial_kernel, nch=nch),
        out_shape=jax.ShapeDtypeStruct((s, d), BF16),
        grid=(g, s // rows),
        in_specs=[pl.BlockSpec((1, CHUNK, CHUNK), lambda gi, ci: (gi, 0, 0)),
                  pl.BlockSpec((1, CHUNK, gd), lambda gi, ci: (gi, 0, 0)),
                  pl.BlockSpec((rows, gd), lambda gi, ci: (ci, gi)),
                  pl.BlockSpec((rows, gd), lambda gi, ci: (ci, gi))],
        out_specs=pl.BlockSpec((rows, gd), lambda gi, ci: (ci, gi)),
        compiler_params=_params(("parallel", "parallel"), 16 << 20),
        name="spatial_gate",
    )(w_sp.astype(F32), bias, vn, z)


def _norm_router_kernel(h_ref, g_ref, wr_ref, xn_ref, lg_ref, *, eps):
    x = h_ref[...]
    ms = jnp.mean(x * x, axis=-1, keepdims=True)
    xn = x * lax.rsqrt(ms + eps) * g_ref[...]
    xn_ref[...] = xn
    x_hi = xn.astype(BF16)
    x_lo = (xn - x_hi.astype(F32)).astype(BF16)
    w = wr_ref[...]
    w_hi = w.astype(BF16)
    w_lo = (w - w_hi.astype(F32)).astype(BF16)
    lg_ref[...] = (jnp.dot(x_hi, w_hi, preferred_element_type=F32) + jnp.dot(x_hi, w_lo, preferred_element_type=F32)
                   + jnp.dot(x_lo, w_hi, preferred_element_type=F32))


def norm_router(h, g, w_router):
    m, d = h.shape
    e = w_router.shape[1]
    wr = jnp.pad(w_router.astype(F32), ((0, 0), (0, LANES - e)))
    tm = _pick(m, (512, 256, 128, 64, 32, 16, 8))
    vm = 2 * (2 * _nbytes((tm, d), F32) + _nbytes((d, LANES), F32)) + 6 * _nbytes((tm, d), F32)
    xn, lg = pl.pallas_call(
        functools.partial(_norm_router_kernel, eps=RMS_EPS),
        out_shape=(jax.ShapeDtypeStruct((m, d), F32), jax.ShapeDtypeStruct((m, LANES), F32)),
        grid=(m // tm,),
        in_specs=[pl.BlockSpec((tm, d), lambda i: (i, 0)),
                  pl.BlockSpec((1, d), lambda i: (0, 0)),
                  pl.BlockSpec((d, LANES), lambda i: (0, 0))],
        out_specs=(pl.BlockSpec((tm, d), lambda i: (i, 0)), pl.BlockSpec((tm, LANES), lambda i: (i, 0))),
        compiler_params=_params(("parallel",), vm),
        name="norm_router",
    )(h, g.reshape(1, d).astype(F32), wr)
    return xn, lg[:, :e]


def _row_copy(src_hbm, src_row, dst, dst_row, sem):
    return pltpu.make_async_copy(src_hbm.at[pl.ds(src_row, 1)], dst.at[pl.ds(dst_row, 1)], sem)


def _gather_rows_kernel(idx_ref, x_hbm, o_hbm, sem, *, rows):
    base = pl.program_id(0) * rows

    def start(r, carry):
        _row_copy(x_hbm, idx_ref[base + r], o_hbm, base + r, sem).start()
        return carry

    lax.fori_loop(0, rows, start, 0)

    def wait(r, carry):
        _row_copy(x_hbm, 0, o_hbm, base + r, sem).wait()
        return carry

    lax.fori_loop(0, rows, wait, 0)


def gather_rows(x, idx):
    m = idx.shape[0]
    d = x.shape[1]
    rows = _pick(m, (256, 128, 64, 32, 16, 8))
    return pl.pallas_call(
        functools.partial(_gather_rows_kernel, rows=rows),
        out_shape=jax.ShapeDtypeStruct((m, d), x.dtype),
        grid_spec=pltpu.PrefetchScalarGridSpec(
            num_scalar_prefetch=1, grid=(m // rows,),
            in_specs=[pl.BlockSpec(memory_space=pl.ANY)],
            out_specs=pl.BlockSpec(memory_space=pl.ANY),
            scratch_shapes=[pltpu.SemaphoreType.DMA(())]),
        compiler_params=pltpu.CompilerParams(dimension_semantics=("arbitrary",)),
        name="moe_gather",
    )(idx, x)


def _moe_up_kernel(te_ref, tv_ref, x_ref, wg_ref, wu_ref, gate_ref, o_ref):
    t = pl.program_id(0)

    @pl.when(tv_ref[t] == 1)
    def _():
        x = x_ref[...].astype(BF16)
        g = jnp.dot(x, wg_ref[...], preferred_element_type=F32)
        u = jnp.dot(x, wu_ref[...], preferred_element_type=F32)
        o_ref[...] = (gate_ref[...] * (g * jax.nn.sigmoid(g) * u)).astype(o_ref.dtype)

    @pl.when(tv_ref[t] == 0)
    def _():
        o_ref[...] = jnp.zeros_like(o_ref)


def _moe_down_kernel(te_ref, tv_ref, a_ref, w_ref, o_ref):
    t = pl.program_id(0)

    @pl.when(tv_ref[t] == 1)
    def _():
        o_ref[...] = jnp.dot(a_ref[...], w_ref[...], preferred_element_type=F32)

    @pl.when(tv_ref[t] == 0)
    def _():
        o_ref[...] = jnp.zeros_like(o_ref)


def _moe_combine_kernel(s1_ref, s2_ref, h_ref, d_hbm, o_ref, buf, sem, *, tm):
    base = pl.program_id(0) * tm

    def start(r, carry):
        _row_copy(d_hbm, s1_ref[base + r], buf.at[0], r, sem).start()
        _row_copy(d_hbm, s2_ref[base + r], buf.at[1], r, sem).start()
        return carry

    lax.fori_loop(0, tm, start, 0)

    def wait(r, carry):
        _row_copy(d_hbm, 0, buf.at[0], r, sem).wait()
        _row_copy(d_hbm, 0, buf.at[1], r, sem).wait()
        return carry

    lax.fori_loop(0, tm, wait, 0)
    o_ref[...] = h_ref[...] + buf[0] + buf[1]


def _route(logits, tile):
    s, ne = logits.shape
    top_val, top_idx = lax.top_k(logits, TOP_K)
    top_w = jax.nn.softmax(top_val, axis=-1)
    e_flat = top_idx.reshape(-1).astype(jnp.int32)
    onehot = (e_flat[:, None] == jnp.arange(ne, dtype=jnp.int32)[None, :]).astype(jnp.int32)
    csum = jnp.cumsum(onehot, axis=0)
    rank = jnp.sum(csum * onehot, axis=1) - 1
    counts = csum[-1]
    padded = (counts + tile - 1) // tile * tile
    ends = jnp.cumsum(padded)
    slot = (ends - padded)[e_flat] + rank
    p_rows = s * TOP_K + ne * tile
    tok = jnp.arange(s * TOP_K, dtype=jnp.int32) // TOP_K
    src = jnp.zeros((p_rows,), jnp.int32).at[slot].set(tok)
    gate = jnp.zeros((p_rows,), F32).at[slot].set(top_w.reshape(-1))
    tstart = jnp.arange(p_rows // tile, dtype=jnp.int32) * tile
    tile_e = jnp.minimum(jnp.sum((tstart[:, None] >= ends[None, :]).astype(jnp.int32), axis=1), ne - 1)
    tile_valid = (tstart < ends[-1]).astype(jnp.int32)
    return src, gate.reshape(p_rows, 1), tile_e, tile_valid, slot.reshape(s, TOP_K)


def _rw_prep_kernel(yr_ref, yk_ref, yv_ref, yl_ref, pr_ref, pk_ref, pv_ref, pl_ref,
                    mur_ref, muk_ref, muv_ref, mul_ref, w2_ref, a2_ref, g2_ref, w0_ref, a0_ref,
                    r_ref, k_ref, v_ref, lw_ref, ai_ref, g_ref):
    first = pl.program_id(0) == 0

    def shifted(x_ref, p_ref, mu_ref):
        x = x_ref[...].astype(F32)
        prev = jnp.where(first, 0.0, p_ref[...].astype(F32)[7:8, :])
        row = lax.broadcasted_iota(jnp.int32, x.shape, 0)
        xp = jnp.where(row == 0, prev, pltpu.roll(x, 1, axis=0))
        return x + (xp - x) * mu_ref[...]

    r_ref[...] = shifted(yr_ref, pr_ref, mur_ref)
    k_ref[...] = shifted(yk_ref, pk_ref, muk_ref)
    v_ref[...] = shifted(yv_ref, pv_ref, muv_ref)
    lo = shifted(yl_ref, pl_ref, mul_ref)
    lane = lax.broadcasted_iota(jnp.int32, lo.shape, 1)
    act = jnp.where(lane < DECAY_LORA, jnp.tanh(lo),
                    jnp.where(lane < DECAY_LORA + AAA_LORA, lo,
                              jnp.where(lane < DECAY_LORA + AAA_LORA + GATE_LORA, jax.nn.sigmoid(lo), 0.0)))
    act = act.astype(BF16)
    w_pre = w0_ref[...] + jnp.dot(act, w2_ref[...], preferred_element_type=F32)
    x = -w_pre
    softplus = jnp.maximum(x, 0.0) + jnp.log(1.0 + jnp.exp(-jnp.abs(x)))
    lw_ref[...] = -jnp.exp(-softplus - 0.5)
    ai_ref[...] = jax.nn.sigmoid(a0_ref[...] + jnp.dot(act, a2_ref[...], preferred_element_type=F32))
    g_ref[...] = jnp.dot(act, g2_ref[...], preferred_element_type=F32)


def _rwkv_group(y, col_lo, mu, w0, w2, a0, a2, g2, k_k, k_a, r_k, lnx_w, lnx_b):
    s = y.shape[0]
    dim = RW_DIM
    lora = DECAY_LORA + AAA_LORA + GATE_LORA
    lwid = -(-lora // LANES) * LANES
    assert col_lo % lwid == 0 and dim % LANES == 0
    tm = _pick(s, (128, 64, 32, 16, 8))
    pb = tm // 8
    mu_rkv = mu[:3 * dim].reshape(1, 3 * dim)
    mu_lo = jnp.pad(mu[3 * dim:], (0, lwid - lora)).reshape(1, lwid)

    def pad_rows(w, start):
        return jnp.pad(w, ((start, lwid - start - w.shape[0]), (0, 0))).astype(BF16)

    y_spec = lambda c: pl.BlockSpec((tm, dim), lambda i: (i, c))
    p_spec = lambda c: pl.BlockSpec((8, dim), lambda i: (jnp.maximum(i * pb - 1, 0), c))
    mu_spec = lambda c: pl.BlockSpec((1, dim), lambda i: (0, c))
    cl = col_lo // lwid
    w_spec = pl.BlockSpec((lwid, dim), lambda i: (0, 0))
    row_spec = pl.BlockSpec((1, dim), lambda i: (0, 0))
    out_spec = pl.BlockSpec((tm, dim), lambda i: (i, 0))
    out_sds = jax.ShapeDtypeStruct((s, dim), F32)
    vm = 2 * (3 * _nbytes((tm, dim), BF16) + 6 * _nbytes((tm, dim), F32) + 3 * _nbytes((lwid, dim), BF16))
    vm += 6 * _nbytes((tm, dim), F32)
    r, k, v, lw, ai, g = pl.pallas_call(
        _rw_prep_kernel,
        out_shape=(out_sds,) * 6,
        grid=(s // tm,),
        in_specs=[y_spec(0), y_spec(1), y_spec(2), pl.BlockSpec((tm, lwid), lambda i: (i, cl)),
                  p_spec(0), p_spec(1), p_spec(2),
                  pl.BlockSpec((8, lwid), lambda i: (jnp.maximum(i * pb - 1, 0), cl)),
                  mu_spec(0), mu_spec(1), mu_spec(2), pl.BlockSpec((1, lwid), lambda i: (0, 0)),
                  w_spec, w_spec, w_spec, row_spec, row_spec],
        out_specs=(out_spec,) * 6,
        compiler_params=_params(("parallel",), vm),
        name="rwkv_prep",
    )(y, y, y, y, y, y, y, y, mu_rkv, mu_rkv, mu_rkv, mu_lo,
      pad_rows(w2, 0), pad_rows(a2, DECAY_LORA), pad_rows(g2, DECAY_LORA + AAA_LORA),
      w0.reshape(1, dim), a0.reshape(1, dim))
    return rwkv_mix(r, k, v, ai, lw, g, k_k, k_a, r_k.reshape(-1), lnx_w, lnx_b)


def _rope(x, cos, sin):
    half = x.shape[-1] // 2
    x1, x2 = x[..., :half], x[..., half:]
    return jnp.concatenate([x1 * cos - x2 * sin, x2 * cos + x1 * sin], axis=-1)


def _mla_group(y, col_q, col_kv, col_pe, cos, sin, q_norm, w_uq, kv_norm, w_ukv):
    s = y.shape[0]
    nh = MLA_HEADS
    cqn = rmsnorm(y, q_norm, BF16, width=Q_LORA, col_block=col_q // Q_LORA)
    ckvn = rmsnorm(y, kv_norm, BF16, width=KV_LORA, col_block=col_kv // KV_LORA)
    assert QK_NOPE == LANES and V_HEAD == LANES and QK_ROPE <= LANES
    half = QK_ROPE // 2
    scale = (QK_NOPE + QK_ROPE) ** -0.5 * 1.4426950408889634
    wq = w_uq.reshape(Q_LORA, nh, QK_NOPE + QK_ROPE)
    w_nope = wq[:, :, :QK_NOPE].reshape(Q_LORA, nh * LANES).astype(BF16)
    w_pe = wq[:, :, QK_NOPE:]
    w_rot = jnp.concatenate([-w_pe[..., half:], w_pe[..., :half]], axis=-1)
    lane_pad = ((0, 0), (0, 0), (0, LANES - QK_ROPE))
    w_pe = jnp.pad(w_pe, lane_pad).reshape(Q_LORA, nh * LANES).astype(BF16)
    w_rot = jnp.pad(w_rot, lane_pad).reshape(Q_LORA, nh * LANES).astype(BF16)
    tm = _pick(s, (1024, 512, 256, 128, 64, 32, 16, 8))
    tn = _pick(nh * LANES, (512, 256, 128))
    table_pad = ((0, 0), (0, LANES - QK_ROPE))
    cos_t = jnp.tile(jnp.pad(jnp.concatenate([cos, cos], axis=-1), table_pad), (1, tn // LANES))
    sin_t = jnp.tile(jnp.pad(jnp.concatenate([sin, sin], axis=-1), table_pad), (1, tn // LANES))
    q_nope = matmul(cqn, [w_nope], lambda accs, extras: accs[0] * scale, BF16, tm=tm, tn=tn, name="mla_q_nope_up")
    q_pe = matmul(cqn, [w_pe, w_rot], lambda accs, extras: (accs[0] * extras[0] + accs[1] * extras[1]) * scale,
                  BF16, tm=tm, tn=tn,
                  extras=[(cos_t, (tm, tn), lambda i, jn: (i, 0)), (sin_t, (tm, tn), lambda i, jn: (i, 0))],
                  name="mla_q_rope_up")
    w_kv = jnp.transpose(w_ukv.reshape(KV_LORA, nh, 2, LANES), (0, 2, 1, 3)).reshape(KV_LORA, 2 * nh * LANES)
    kv = matmul(ckvn, [w_kv.astype(BF16)], _epi_id, BF16, tm=tm, tn=tn, name="mla_kv_up")
    k_pe = _rope(y[:, col_pe:col_pe + QK_ROPE].astype(F32), cos, sin)
    k_pe = jnp.pad(k_pe, table_pad).astype(BF16)
    return flash_attention(q_nope, q_pe, kv, k_pe, nh)


def _even_layer(h, hn, j, cos, sin, prm):
    w_in = prm["ev_in_proj"][j]
    rw_end = 3 * RW_DIM
    lora_w = DECAY_LORA + AAA_LORA + GATE_LORA
    rw_proj = rw_end + lora_w
    w_perm = jnp.concatenate([
        w_in[:, :rw_end],
        w_in[:, rw_proj:rw_proj + Q_LORA + KV_LORA],
        w_in[:, rw_end:rw_proj],
        w_in[:, rw_proj + Q_LORA + KV_LORA:],
    ], axis=1).astype(BF16)
    y = matmul(hn, [w_perm], _epi_id, BF16, tm=1024, tn=512, name="even_in_proj")
    col_q = rw_end
    col_kv = rw_end + Q_LORA
    col_lo = rw_end + Q_LORA + KV_LORA
    col_pe = col_lo + lora_w
    mu = prm["rw_mu"][j]
    o_rw = _rwkv_group(y, col_lo, mu, prm["rw_w0"][j], prm["rw_w2"][j], prm["rw_a0"][j], prm["rw_a2"][j],
                       prm["rw_g2"][j], prm["rw_k_k"][j], prm["rw_k_a"][j], prm["rw_r_k"][j],
                       prm["rw_lnx_w"][j], prm["rw_lnx_b"][j])
    o_mla = _mla_group(y, col_q, col_kv, col_pe, cos, sin, prm["mla_q_norm"][j], prm["mla_w_uq"][j],
                       prm["mla_kv_norm"][j], prm["mla_w_ukv"][j])
    mix = jnp.concatenate([o_rw, o_mla], axis=-1)
    tm, tn = 1024, 512
    h = matmul(mix, [prm["ev_out_proj"][j].astype(BF16)], _epi_residual, F32, tm=tm, tn=tn,
               extras=[_res_extra(h, _pick(h.shape[0], (tm, 512, 256, 128, 64, 32, 16, 8)),
                                  _pick(h.shape[1], (tn, 512, 256, 128)), 1)], name="even_out_proj")
    return h


def _dense_ffn(h, hn, wg, wu, wd):
    d, f = wg.shape
    fp = -(-f // 1024) * 1024 if f > 1024 else f
    pad = fp - f
    wg = jnp.pad(wg.astype(BF16), ((0, 0), (0, pad)))
    wu = jnp.pad(wu.astype(BF16), ((0, 0), (0, pad)))
    wd = jnp.pad(wd.astype(BF16), ((0, pad), (0, 0)))
    mid = matmul(hn, [wg, wu], _epi_swiglu, BF16, tm=1024, tn=512, name="ffn_up")
    tm = _pick(h.shape[0], (1024, 512, 256, 128, 64, 32, 16, 8))
    tn = _pick(h.shape[1], (1024, 512, 256, 128))
    tk = _pick(fp, (2816, 2048, 1024, 512, 256, 128))
    nk = fp // tk
    return matmul(mid, [wd], _epi_residual, F32, tm=tm, tn=tn, tk=tk,
                  extras=[_res_extra(h, tm, tn, nk)], name="ffn_down")


def _moe(h, g_norm, w_router, wg, wu, wd):
    ne, d, f = wg.shape
    s = h.shape[0]
    xn, logits = norm_router(h, g_norm, w_router)
    tile = _pick(s, (MOE_TILE, 256, 128, 64, 32, 16, 8))
    src, gate, tile_e, tile_valid, slots = _route(logits, tile)
    p_rows = src.shape[0]
    n_tiles = p_rows // tile
    xs = gather_rows(xn, src)
    tn = _pick(f, (512, 256, 128))
    vm = 2 * (_nbytes((tile, d), F32) + 2 * _nbytes((d, tn), BF16) + _nbytes((tile, tn), BF16))
    vm += _nbytes((tile, d), BF16) + 4 * _nbytes((tile, tn), F32)
    w_spec = pl.BlockSpec((None, d, tn), lambda t, jn, te, tv: (te[t], 0, jn))
    mid = pl.pallas_call(
        _moe_up_kernel,
        out_shape=jax.ShapeDtypeStruct((p_rows, f), BF16),
        grid_spec=pltpu.PrefetchScalarGridSpec(
            num_scalar_prefetch=2, grid=(n_tiles, f // tn),
            in_specs=[pl.BlockSpec((tile, d), lambda t, jn, te, tv: (t, 0)), w_spec, w_spec,
                      pl.BlockSpec((tile, 1), lambda t, jn, te, tv: (t, 0))],
            out_specs=pl.BlockSpec((tile, tn), lambda t, jn, te, tv: (t, jn))),
        compiler_params=_params(("parallel", "parallel"), vm),
        name="moe_up",
    )(tile_e, tile_valid, xs, wg.astype(BF16), wu.astype(BF16), gate)
    tn2 = _pick(d, (1024, 512, 256, 128))
    vm = 2 * (_nbytes((tile, f), BF16) + _nbytes((f, tn2), BF16) + _nbytes((tile, tn2), F32)) + _nbytes((tile, tn2), F32)
    dn = pl.pallas_call(
        _moe_down_kernel,
        out_shape=jax.ShapeDtypeStruct((p_rows, d), F32),
        grid_spec=pltpu.PrefetchScalarGridSpec(
            num_scalar_prefetch=2, grid=(n_tiles, d // tn2),
            in_specs=[pl.BlockSpec((tile, f), lambda t, jn, te, tv: (t, 0)),
                      pl.BlockSpec((None, f, tn2), lambda t, jn, te, tv: (te[t], 0, jn))],
            out_specs=pl.BlockSpec((tile, tn2), lambda t, jn, te, tv: (t, jn))),
        compiler_params=_params(("parallel", "parallel"), vm),
        name="moe_down",
    )(tile_e, tile_valid, mid, wd.astype(BF16))
    tm = _pick(s, (256, 128, 64, 32, 16, 8))
    vm = 4 * _nbytes((tm, d), F32) + 2 * _nbytes((tm, d), F32) + 2 * _nbytes((tm, d), F32)
    return pl.pallas_call(
        functools.partial(_moe_combine_kernel, tm=tm),
        out_shape=jax.ShapeDtypeStruct((s, d), F32),
        grid_spec=pltpu.PrefetchScalarGridSpec(
            num_scalar_prefetch=2, grid=(s // tm,),
            in_specs=[pl.BlockSpec((tm, d), lambda i, s1, s2: (i, 0)), pl.BlockSpec(memory_space=pl.ANY)],
            out_specs=pl.BlockSpec((tm, d), lambda i, s1, s2: (i, 0)),
            scratch_shapes=[pltpu.VMEM((2, tm, d), F32), pltpu.SemaphoreType.DMA(())]),
        compiler_params=_params(("arbitrary",), vm),
        name="moe_combine",
    )(slots[:, 0], slots[:, 1], h, dn)


def _odd_layer(h, hn, j, prm):
    z = matmul(hn, [prm["sg_in_proj"][j].astype(BF16)], _epi_gelu, BF16, tm=1024, tn=512, name="sg_in_proj")
    d = h.shape[1]
    vn = layernorm(z, prm["sg_ln_w"][j], prm["sg_ln_b"][j], BF16, width=d, col_block=1)
    gated = spatial_gate(z, vn, prm["sg_w_spatial"][j], prm["sg_b_spatial"][j])
    tm = _pick(h.shape[0], (1024, 512, 256, 128, 64, 32, 16, 8))
    tn = _pick(d, (512, 256, 128))
    return matmul(gated, [prm["sg_out_proj"][j].astype(BF16)], _epi_residual, F32, tm=tm, tn=tn,
                  extras=[_res_extra(h, tm, tn, 1)], name="sg_out_proj")


def kernel(x, p, positions, norm_mix, norm_ffn, norm_ple, final_norm, ev_in_proj, rw_mu, rw_w0, rw_w2, rw_a0, rw_a2, rw_g2, rw_k_k, rw_k_a, rw_r_k, rw_lnx_w, rw_lnx_b, mla_q_norm, mla_w_uq, mla_kv_norm, mla_w_ukv, ev_out_proj, ffn_w_gate, ffn_w_up, ffn_w_down, sg_in_proj, sg_ln_w, sg_ln_b, sg_w_spatial, sg_b_spatial, sg_out_proj, moe_router, moe_w_gate, moe_w_up, moe_w_down, ple_proj, ple_gate):
    prm = dict(ev_in_proj=ev_in_proj, rw_mu=rw_mu, rw_w0=rw_w0, rw_w2=rw_w2, rw_a0=rw_a0, rw_a2=rw_a2, rw_g2=rw_g2,
               rw_k_k=rw_k_k, rw_k_a=rw_k_a, rw_r_k=rw_r_k, rw_lnx_w=rw_lnx_w, rw_lnx_b=rw_lnx_b,
               mla_q_norm=mla_q_norm, mla_w_uq=mla_w_uq, mla_kv_norm=mla_kv_norm, mla_w_ukv=mla_w_ukv,
               ev_out_proj=ev_out_proj, sg_in_proj=sg_in_proj, sg_ln_w=sg_ln_w, sg_ln_b=sg_ln_b,
               sg_w_spatial=sg_w_spatial, sg_b_spatial=sg_b_spatial, sg_out_proj=sg_out_proj)
    bsz, s, d = x.shape
    assert bsz == 1
    inv = 1.0 / (ROPE_THETA ** (jnp.arange(0, QK_ROPE, 2, dtype=F32) / QK_ROPE))
    ang = positions[0].astype(F32)[:, None] * inv
    cos, sin = jnp.cos(ang), jnp.sin(ang)
    h = x[0]
    for i in range(DEPTH):
        j = i // 2
        if i % 2 == 0:
            hn = rmsnorm(h, norm_mix[i], BF16)
            h = _even_layer(h, hn, j, cos, sin, prm)
            hn = rmsnorm(h, norm_ffn[i], BF16)
            h = _dense_ffn(h, hn, ffn_w_gate[j], ffn_w_up[j], ffn_w_down[j])
        else:
            hn = rmsnorm(h, norm_mix[i], BF16)
            h = _odd_layer(h, hn, j, prm)
            h = _moe(h, norm_ffn[i], moe_router[j], moe_w_gate[j], moe_w_up[j], moe_w_down[j])
        hn = rmsnorm(h, norm_ple[i], BF16)
        h = ple(hn, ple_gate[i].astype(BF16), p[i, 0].astype(BF16), ple_proj[i].astype(BF16), h)
    return rmsnorm(h, final_norm, F32)[None]
```

```python
import functools

import jax
import jax.numpy as jnp
from jax import lax
from jax.experimental import pallas as pl
from jax.experimental.pallas import tpu as pltpu

D_MODEL = 4096
DEPTH = 4
RW_HEADS = 32
RW_HEAD = 64
RW_DIM = RW_HEADS * RW_HEAD
DECAY_LORA = 96
AAA_LORA = 96
GATE_LORA = 256
GN_EPS = 64e-5
MLA_HEADS = 16
Q_LORA = 1536
KV_LORA = 512
QK_NOPE = 128
QK_ROPE = 64
V_HEAD = 128
ROPE_THETA = 10000.0
CHUNK = 128
SG_GROUPS = 32
LN_EPS = 1e-5
D_FF = 11008
N_EXPERTS = 8
TOP_K = 2
D_FF_EXPERT = 1536
PLE_DIM = 256
RMS_EPS = 1e-6

V7X_VMEM_BYTES = 64 * 1024 * 1024
LANES = 128
RW_CHUNK = 64
RW_HEADS_PER_STEP = 8
RW_CHUNKS_PER_STEP = 2
FLASH_T = 1024
MOE_TILE = 512

F32 = jnp.float32
BF16 = jnp.bfloat16


def _pick(dim, candidates):
    for c in candidates:
        if dim % c == 0:
            return c
    return dim


def _params(sem, vmem_bytes, flags=None):
    limit = int(min(V7X_VMEM_BYTES - 6 * 1024 * 1024, max(vmem_bytes * 5 // 4 + (2 << 20), 16 << 20)))
    return pltpu.CompilerParams(dimension_semantics=sem, vmem_limit_bytes=limit, flags=flags)


def _nbytes(shape, dtype):
    n = 1
    for s in shape:
        n *= s
    return n * jnp.dtype(dtype).itemsize


def _rmsnorm_kernel(x_ref, g_ref, o_ref, *, eps):
    x = x_ref[...].astype(F32)
    ms = jnp.mean(x * x, axis=-1, keepdims=True)
    o_ref[...] = (x * lax.rsqrt(ms + eps) * g_ref[...]).astype(o_ref.dtype)


def rmsnorm(x, g, out_dtype, *, width=None, col_block=0):
    m = x.shape[0]
    d = x.shape[1] if width is None else width
    tm = _pick(m, (512, 256, 128, 64, 32, 16, 8))
    vm = 2 * (_nbytes((tm, d), x.dtype) + _nbytes((tm, d), out_dtype)) + 4 * _nbytes((tm, d), F32)
    return pl.pallas_call(
        functools.partial(_rmsnorm_kernel, eps=RMS_EPS),
        out_shape=jax.ShapeDtypeStruct((m, d), out_dtype),
        grid=(m // tm,),
        in_specs=[pl.BlockSpec((tm, d), lambda i: (i, col_block)),
                  pl.BlockSpec((1, d), lambda i: (0, 0))],
        out_specs=pl.BlockSpec((tm, d), lambda i: (i, 0)),
        compiler_params=_params(("parallel",), vm),
        name="rmsnorm",
    )(x, g.reshape(1, d).astype(F32))


def _layernorm_kernel(x_ref, w_ref, b_ref, o_ref, *, eps):
    x = x_ref[...].astype(F32)
    mean = jnp.mean(x, axis=-1, keepdims=True)
    var = jnp.mean(jnp.square(x - mean), axis=-1, keepdims=True)
    o_ref[...] = ((x - mean) * lax.rsqrt(var + eps) * w_ref[...] + b_ref[...]).astype(o_ref.dtype)


def layernorm(x, w, b, out_dtype, *, width, col_block):
    m = x.shape[0]
    tm = _pick(m, (512, 256, 128, 64, 32, 16, 8))
    vm = 2 * (_nbytes((tm, width), x.dtype) + _nbytes((tm, width), out_dtype)) + 4 * _nbytes((tm, width), F32)
    return pl.pallas_call(
        functools.partial(_layernorm_kernel, eps=LN_EPS),
        out_shape=jax.ShapeDtypeStruct((m, width), out_dtype),
        grid=(m // tm,),
        in_specs=[pl.BlockSpec((tm, width), lambda i: (i, col_block)),
                  pl.BlockSpec((1, width), lambda i: (0, 0)),
                  pl.BlockSpec((1, width), lambda i: (0, 0))],
        out_specs=pl.BlockSpec((tm, width), lambda i: (i, 0)),
        compiler_params=_params(("parallel",), vm),
        name="layernorm",
    )(x, w.reshape(1, width).astype(F32), b.reshape(1, width).astype(F32))


def _mm_kernel(*refs, nk, n_w, n_extra, epi):
    a_ref = refs[0]
    w_refs = refs[1:1 + n_w]
    extra_refs = refs[1 + n_w:1 + n_w + n_extra]
    o_ref = refs[1 + n_w + n_extra]
    acc_refs = refs[2 + n_w + n_extra:]
    a = a_ref[...]
    if nk == 1:
        accs = [jnp.dot(a, w[...], preferred_element_type=F32) for w in w_refs]
        o_ref[...] = epi(accs, [e[...] for e in extra_refs]).astype(o_ref.dtype)
        return
    k = pl.program_id(2)

    @pl.when(k == 0)
    def _():
        for acc in acc_refs:
            acc[...] = jnp.zeros_like(acc)

    for acc, w in zip(acc_refs, w_refs):
        acc[...] += jnp.dot(a, w[...], preferred_element_type=F32)

    @pl.when(k == nk - 1)
    def _():
        o_ref[...] = epi([acc[...] for acc in acc_refs], [e[...] for e in extra_refs]).astype(o_ref.dtype)


def matmul(a, ws, epi, out_dtype, *, extras=(), tm=1024, tn=512, tk=None, a_col_block=0, a_width=None,
           w_specs=None, n_out=None, name="matmul"):
    m = a.shape[0]
    kdim = a.shape[1] if a_width is None else a_width
    n = ws[0].shape[-1] if n_out is None else n_out
    tm = _pick(m, (tm, 512, 256, 128, 64, 32, 16, 8))
    tn = _pick(n, (tn, 512, 256, 128))
    tk = kdim if tk is None else _pick(kdim, (tk,))
    nk = kdim // tk
    kb0 = a_col_block * nk
    if nk == 1:
        grid = (m // tm, n // tn)
        a_spec = pl.BlockSpec((tm, tk), lambda i, j: (i, kb0))
        w_spec_list = w_specs or [pl.BlockSpec((tk, tn), lambda i, j: (0, j)) for _ in ws]
        o_spec = pl.BlockSpec((tm, tn), lambda i, j: (i, j))
        sem = ("parallel", "parallel")
        scratch = []
    else:
        grid = (m // tm, n // tn, nk)
        a_spec = pl.BlockSpec((tm, tk), lambda i, j, k: (i, kb0 + k))
        w_spec_list = w_specs or [pl.BlockSpec((tk, tn), lambda i, j, k: (k, j)) for _ in ws]
        o_spec = pl.BlockSpec((tm, tn), lambda i, j, k: (i, j))
        sem = ("parallel", "parallel", "arbitrary")
        scratch = [pltpu.VMEM((tm, tn), F32) for _ in ws]
    extra_specs = [pl.BlockSpec(bs, im) for (_, bs, im) in extras]
    vm = 2 * _nbytes((tm, tk), a.dtype) + 2 * len(ws) * _nbytes((tk, tn), ws[0].dtype)
    vm += 2 * _nbytes((tm, tn), out_dtype) + (len(ws) + 2) * _nbytes((tm, tn), F32)
    for (arr, bs, _) in extras:
        vm += 2 * _nbytes([1 if b is None else b for b in bs], arr.dtype)
    return pl.pallas_call(
        functools.partial(_mm_kernel, nk=nk, n_w=len(ws), n_extra=len(extras), epi=epi),
        out_shape=jax.ShapeDtypeStruct((m, n), out_dtype),
        grid=grid,
        in_specs=[a_spec] + list(w_spec_list) + extra_specs,
        out_specs=o_spec,
        scratch_shapes=scratch,
        compiler_params=_params(sem, vm),
        name=name,
    )(a, *ws, *[e[0] for e in extras])


def _epi_id(accs, extras):
    return accs[0]


def _epi_residual(accs, extras):
    return extras[0] + accs[0]


def _epi_swiglu(accs, extras):
    g, u = accs
    return g * jax.nn.sigmoid(g) * u


def _epi_gelu(accs, extras):
    z = accs[0]
    return 0.5 * z * (1.0 + lax.erf(z * (2.0 ** -0.5)))


def _res_extra(h, tm, tn, nk):
    if nk == 1:
        return (h, (tm, tn), lambda i, j: (i, j))
    return (h, (tm, tn), lambda i, j, k: (i, j))


def _ple_kernel(hn_ref, wg_ref, p_ref, wp_ref, h_ref, o_ref):
    gate = jax.nn.sigmoid(jnp.dot(hn_ref[...], wg_ref[...], preferred_element_type=F32))
    proj = jnp.dot(p_ref[...], wp_ref[...], preferred_element_type=F32)
    o_ref[...] = h_ref[...] + gate * proj


def ple(hn, w_gate, p, w_proj, h):
    m, d = hn.shape
    n = w_gate.shape[1]
    pd = p.shape[1]
    tm = _pick(m, (1024, 512, 256, 128, 64, 32, 16, 8))
    tn = _pick(n, (512, 256, 128))
    vm = 2 * (_nbytes((tm, d), BF16) + _nbytes((d, tn), BF16) + _nbytes((tm, pd), BF16) + _nbytes((pd, tn), BF16))
    vm += 6 * _nbytes((tm, tn), F32)
    return pl.pallas_call(
        _ple_kernel,
        out_shape=jax.ShapeDtypeStruct((m, n), F32),
        grid=(m // tm, n // tn),
        in_specs=[pl.BlockSpec((tm, d), lambda i, j: (i, 0)),
                  pl.BlockSpec((d, tn), lambda i, j: (0, j)),
                  pl.BlockSpec((tm, pd), lambda i, j: (i, 0)),
                  pl.BlockSpec((pd, tn), lambda i, j: (0, j)),
                  pl.BlockSpec((tm, tn), lambda i, j: (i, j))],
        out_specs=pl.BlockSpec((tm, tn), lambda i, j: (i, j)),
        compiler_params=_params(("parallel", "parallel"), vm),
        name="ple",
    )(hn, w_gate, p, w_proj, h)


def _bdot(a, b):
    return jnp.dot(a.astype(BF16), b.astype(BF16), preferred_element_type=F32)


def _bdot_nt(a, b):
    return lax.dot_general(a.astype(BF16), b.astype(BF16), (((1,), (1,)), ((), ())),
                           preferred_element_type=F32)


def _bdot_tn(a, b):
    return lax.dot_general(a.astype(BF16), b.astype(BF16), (((0,), (0,)), ((), ())),
                           preferred_element_type=F32)


def _split3(x):
    hi = x.astype(BF16)
    r1 = x - hi.astype(F32)
    mid = r1.astype(BF16)
    lo = (r1 - mid.astype(F32)).astype(BF16)
    return hi, mid, lo


def _bmm(a, b):
    return jnp.einsum("bij,bjk->bik", a.astype(BF16), b.astype(BF16), preferred_element_type=F32)


def _bmm_nt(a, b):
    return jnp.einsum("bik,bjk->bij", a.astype(BF16), b.astype(BF16), preferred_element_type=F32)


def _bmm_tn(a, b):
    return jnp.einsum("bti,btj->bij", a.astype(BF16), b.astype(BF16), preferred_element_type=F32)


def _rwkv_a_kernel(r_ref, k_ref, v_ref, ai_ref, lw_ref, kk_ref, ka_ref, mt_ref, nt_ref, q_ref, o0_ref, *, hb, ncb):
    t = RW_CHUNK
    n = RW_HEAD
    nb = hb * ncb
    row = lax.broadcasted_iota(jnp.int32, (t, t), 0)
    col = lax.broadcasted_iota(jnp.int32, (t, t), 1)
    incl = col <= row
    strict = col < row
    tri = jnp.broadcast_to(incl.astype(BF16), (nb, t, t))
    eye = (row == col).astype(F32)
    drow = lax.broadcasted_iota(jnp.int32, (n, n), 0)
    dcol = lax.broadcasted_iota(jnp.int32, (n, n), 1)

    def units(ref):
        return jnp.stack([ref[c * t:(c + 1) * t, h * n:(h + 1) * n] for c in range(ncb) for h in range(hb)])

    def unit_params(ref):
        return jnp.stack([ref[:, h * n:(h + 1) * n] for c in range(ncb) for h in range(hb)])

    k = units(k_ref)
    ai = units(ai_ref)
    kk = k * unit_params(kk_ref)
    kk = kk * lax.rsqrt(jnp.maximum(jnp.sum(kk * kk, axis=-1, keepdims=True), 1e-24))
    k = k * (1.0 + (ai - 1.0) * unit_params(ka_ref))
    a = -kk
    b = kk * ai
    lw = units(lw_ref)
    hi, mid, lo = _split3(lw)
    cum = (jnp.einsum("bij,bjk->bik", tri, hi, preferred_element_type=F32)
           + jnp.einsum("bij,bjk->bik", tri, mid, preferred_element_type=F32)
           + jnp.einsum("bij,bjk->bik", tri, lo, preferred_element_type=F32))
    cum_t = cum[:, t - 1:t, :]
    e_in = jnp.exp(cum)
    e_out = jnp.exp(-cum)
    e_prev = jnp.exp(cum - lw)
    e_rem = jnp.exp(cum_t - cum)
    p_t = jnp.exp(cum_t)
    r = units(r_ref)
    v = units(v_ref)
    rh = r * e_in
    ah = a * e_prev
    bh = b * e_out
    kh = k * e_out
    bt = b * e_rem
    kt = k * e_rem
    g = _bmm_nt(jnp.concatenate([ah, rh], axis=1), jnp.concatenate([bh, kh], axis=1))
    row2 = lax.broadcasted_iota(jnp.int32, (t, 2 * t), 0)
    col2 = lax.broadcasted_iota(jnp.int32, (t, 2 * t), 1) % t
    n_cat = jnp.where(col2 < row2, g[:, :t, :], 0.0)
    l_cat = jnp.where(col2 <= row2, g[:, t:, :], 0.0)
    n_ab = n_cat[:, :, :t]
    x = eye + n_ab
    pw = n_ab
    for _ in range((t - 1).bit_length() - 1):
        pw = _bmm(pw, pw)
        x = x + _bmm(x, pw)
    zeros = jnp.zeros_like(v)
    akv = _bmm(n_cat, jnp.concatenate([zeros, v], axis=1))
    wu = _bmm(x, jnp.concatenate([ah, akv], axis=2))
    wu_v = jnp.concatenate([wu, jnp.concatenate([zeros, v], axis=2)], axis=1)
    l2 = _bmm(l_cat, wu_v)
    b2 = _bmm_tn(jnp.concatenate([bt, kt], axis=1), wu_v)
    qt = rh + l2[:, :, :n]
    o0 = l2[:, :, n:]
    mt = jnp.where(drow == dcol, p_t, 0.0) + b2[:, :, :n]
    nt = b2[:, :, n:]
    for c in range(ncb):
        rows = slice(c * t, (c + 1) * t)
        q_ref[rows, :] = jnp.concatenate([qt[c * hb + h] for h in range(hb)], axis=-1)
        o0_ref[rows, :] = jnp.concatenate([o0[c * hb + h] for h in range(hb)], axis=-1)
        mt_ref[:, c] = mt[c * hb:(c + 1) * hb]
        nt_ref[:, c] = nt[c * hb:(c + 1) * hb]


def _bdot3(a, b):
    a_hi = a.astype(BF16)
    a_lo = (a - a_hi.astype(F32)).astype(BF16)
    b_hi = b.astype(BF16)
    b_lo = (b - b_hi.astype(F32)).astype(BF16)
    mm = lambda x, y: jnp.einsum("bij,bjk->bik", x, y, preferred_element_type=F32)
    return mm(a_hi, b_hi) + mm(a_hi, b_lo) + mm(a_lo, b_hi)


def _rwkv_b_kernel(mt_ref, nt_ref, q_ref, o0_ref, r_ref, k_ref, v_ref, ai_ref, g_ref, ka_ref, rk_ref, lw_ref, lb_ref,
                   o_ref, z_ref, *, nh):
    n = RW_HEAD
    c = pl.program_id(0)

    @pl.when(c == 0)
    def _():
        z_ref[...] = jnp.zeros_like(z_ref)

    def heads(ref):
        return jnp.stack([ref[:, h * n:(h + 1) * n] for h in range(nh)])

    z = z_ref[...]
    o = heads(o0_ref) + _bdot3(heads(q_ref), z)
    z_ref[...] = nt_ref[:, 0] + _bdot3(mt_ref[:, 0], z)
    mean = jnp.mean(o, axis=-1, keepdims=True)
    var = jnp.mean(jnp.square(o - mean), axis=-1, keepdims=True)
    o = (o - mean) * lax.rsqrt(var + GN_EPS) * heads(lw_ref) + heads(lb_ref)
    k = heads(k_ref) * (1.0 + (heads(ai_ref) - 1.0) * heads(ka_ref))
    bonus = jnp.sum(heads(r_ref) * k * heads(rk_ref), axis=-1, keepdims=True) * heads(v_ref)
    out = (o + bonus) * heads(g_ref)
    o_ref[...] = jnp.concatenate([out[h] for h in range(nh)], axis=-1).astype(o_ref.dtype)


def rwkv_mix(r, k, v, ai, lw, g, k_k, k_a, r_k, lnx_w, lnx_b):
    s, dim = r.shape
    n = RW_HEAD
    nh = dim // n
    t = RW_CHUNK
    nc = s // t
    hb = _pick(nh, (RW_HEADS_PER_STEP, 4, 2))
    ncb = _pick(nc, (RW_CHUNKS_PER_STEP, 2, 1))
    row = lambda p: p.reshape(1, dim).astype(F32)
    seq_spec = pl.BlockSpec((ncb * t, hb * n), lambda g_, c: (c, g_))
    par_spec = pl.BlockSpec((1, hb * n), lambda g_, c: (0, g_))
    mat_spec = pl.BlockSpec((hb, ncb, n, n), lambda g_, c: (g_, c, 0, 0))
    mt, nt, qt, o0 = pl.pallas_call(
        functools.partial(_rwkv_a_kernel, hb=hb, ncb=ncb),
        out_shape=(jax.ShapeDtypeStruct((nh, nc, n, n), F32), jax.ShapeDtypeStruct((nh, nc, n, n), F32),
                   jax.ShapeDtypeStruct((s, dim), F32), jax.ShapeDtypeStruct((s, dim), F32)),
        grid=(nh // hb, nc // ncb),
        in_specs=[seq_spec] * 5 + [par_spec] * 2,
        out_specs=(mat_spec, mat_spec, seq_spec, seq_spec),
        compiler_params=_params(("parallel", "parallel"), 32 << 20),
        name="rwkv_chunk_local",
    )(r, k, v, ai, lw, row(k_k), row(k_a))
    seq_all = pl.BlockSpec((t, dim), lambda c: (c, 0))
    par_all = pl.BlockSpec((1, dim), lambda c: (0, 0))
    mat_all = pl.BlockSpec((nh, 1, n, n), lambda c: (0, c, 0, 0))
    return pl.pallas_call(
        functools.partial(_rwkv_b_kernel, nh=nh),
        out_shape=jax.ShapeDtypeStruct((s, dim), BF16),
        grid=(nc,),
        in_specs=[mat_all, mat_all] + [seq_all] * 7 + [par_all] * 4,
        out_specs=seq_all,
        scratch_shapes=[pltpu.VMEM((nh, n, n), F32)],
        compiler_params=_params(("arbitrary",), 32 << 20),
        name="rwkv_state_scan",
    )(mt, nt, qt, o0, r, k, v, ai, g, row(k_a), row(r_k), row(lnx_w), row(lnx_b))


def _flash_kernel(qn_ref, qp_ref, kn_ref, kp_ref, v_ref, o_ref, q_sc, m_sc, acc_sc, s0_sc, s1_sc, *, t):
    i = pl.program_id(1)
    nt = t // LANES
    q_sc[...] = jnp.concatenate([qn_ref[...], qp_ref[...]], axis=1)
    m_sc[...] = jnp.full_like(m_sc, -1e30)
    acc_sc[...] = jnp.zeros_like(acc_sc)
    ones = jnp.ones((t, LANES), BF16)

    def scores(j, s_ref):
        start = pl.multiple_of(j * t, t)
        kb = jnp.concatenate([kn_ref[pl.ds(start, t), :], kp_ref[pl.ds(start, t), :]], axis=1)
        s_ref[...] = lax.dot_general(q_sc[...], kb, (((1,), (1,)), ((), ())), preferred_element_type=F32)

    def softmax_pv(j, s_ref, masked):
        start = pl.multiple_of(j * t, t)
        vb = jnp.concatenate([v_ref[pl.ds(start, t), :], ones], axis=1)
        tiles = [s_ref[:, c * LANES:(c + 1) * LANES] for c in range(nt)]
        if masked:
            rows = lax.broadcasted_iota(jnp.int32, (t, LANES), 0)
            cols = lax.broadcasted_iota(jnp.int32, (t, LANES), 1)
            tiles = [jnp.where(cols + c * LANES <= rows, tl, -1e30) for c, tl in enumerate(tiles)]
        mx = tiles[0]
        for c in range(1, nt):
            mx = jnp.maximum(mx, tiles[c])
        m_prev = m_sc[...]
        m_new = jnp.maximum(m_prev, jnp.max(mx, axis=1, keepdims=True))
        alpha = jnp.exp2(m_prev - m_new)
        p = jnp.concatenate([jnp.exp2(tl - m_new).astype(BF16) for tl in tiles], axis=1)
        pv = jnp.dot(p, vb, preferred_element_type=F32)
        acc_sc[...] = jnp.concatenate([alpha, alpha], axis=1) * acc_sc[...] + pv
        m_sc[...] = m_new

    def stage(j, s_cur, s_next):
        scores(j + 1, s_next)
        softmax_pv(j, s_cur, False)

    scores(0, s0_sc)

    def pair(jp, carry):
        stage(2 * jp, s0_sc, s1_sc)
        stage(2 * jp + 1, s1_sc, s0_sc)
        return carry

    lax.fori_loop(0, i // 2, pair, 0)

    @pl.when(i % 2 == 1)
    def _():
        stage(i - 1, s0_sc, s1_sc)
        softmax_pv(i, s1_sc, True)

    @pl.when(i % 2 == 0)
    def _():
        softmax_pv(i, s0_sc, True)

    o_ref[...] = (acc_sc[:, :LANES] / acc_sc[:, LANES:]).astype(o_ref.dtype)


def flash_attention(q_nope, q_pe, kv, k_pe, nh):
    s = q_nope.shape[0]
    dh = LANES
    tq = _pick(s, (FLASH_T, 512, 256, 128))
    vm = 2 * (3 * _nbytes((s, dh), BF16) + 3 * _nbytes((tq, dh), BF16))
    vm += 5 * _nbytes((tq, tq), F32) + 4 * _nbytes((tq, 2 * dh), F32)
    scratch = [pltpu.VMEM((tq, 2 * dh), BF16), pltpu.VMEM((tq, dh), F32), pltpu.VMEM((tq, 2 * dh), F32),
               pltpu.VMEM((tq, tq), F32), pltpu.VMEM((tq, tq), F32)]
    return pl.pallas_call(
        functools.partial(_flash_kernel, t=tq),
        out_shape=jax.ShapeDtypeStruct((s, nh * dh), BF16),
        grid=(nh, s // tq),
        in_specs=[pl.BlockSpec((tq, dh), lambda h, i: (i, h)),
                  pl.BlockSpec((tq, dh), lambda h, i: (i, h)),
                  pl.BlockSpec((s, dh), lambda h, i: (0, h)),
                  pl.BlockSpec((s, dh), lambda h, i: (0, 0)),
                  pl.BlockSpec((s, dh), lambda h, i: (0, nh + h))],
        out_specs=pl.BlockSpec((tq, dh), lambda h, i: (i, h)),
        scratch_shapes=scratch,
        compiler_params=_params(("parallel", "arbitrary"), vm),
        name="mla_flash_attention",
    )(q_nope, q_pe, kv, k_pe, kv)


def _spatial_kernel(w_ref, bias_ref, v_ref, u_ref, o_ref, *, nch):
    c = CHUNK
    row = lax.broadcasted_iota(jnp.int32, (c, c), 0)
    col = lax.broadcasted_iota(jnp.int32, (c, c), 1)
    w = jnp.where(col <= row, w_ref[0], 0.0).astype(BF16)
    bias = bias_ref[0]
    for ci in range(nch):
        sl = pl.ds(ci * c, c)
        sg = jnp.dot(w, v_ref[sl, :], preferred_element_type=F32) + bias
        o_ref[sl, :] = (u_ref[sl, :].astype(F32) * sg).astype(o_ref.dtype)


def spatial_gate(z, vn, w_sp, b_sp):
    s, d = vn.shape
    g = w_sp.shape[0]
    gd = d // g
    nch = _pick(s // CHUNK, (16, 8, 4, 2, 1))
    rows = nch * CHUNK
    bias = jnp.broadcast_to(b_sp.astype(F32)[:, :, None], (g, CHUNK, gd))
    return pl.pallas_call(
        functools.partial(_spatial_kernel, nch=nch),
        out_shape=jax.ShapeDtypeStruct((s, d), BF16),
        grid=(g, s // rows),
        in_specs=[pl.BlockSpec((1, CHUNK, CHUNK), lambda gi, ci: (gi, 0, 0)),
                  pl.BlockSpec((1, CHUNK, gd), lambda gi, ci: (gi, 0, 0)),
                  pl.BlockSpec((rows, gd), lambda gi, ci: (ci, gi)),
                  pl.BlockSpec((rows, gd), lambda gi, ci: (ci, gi))],
        out_specs=pl.BlockSpec((rows, gd), lambda gi, ci: (ci, gi)),
        compiler_params=_params(("parallel", "parallel"), 16 << 20),
        name="spatial_gate",
    )(w_sp.astype(F32), bias, vn, z)


def _norm_router_kernel(h_ref, g_ref, wr_ref, xn_ref, lg_ref, *, eps):
    x = h_ref[...]
    ms = jnp.mean(x * x, axis=-1, keepdims=True)
    xn = x * lax.rsqrt(ms + eps) * g_ref[...]
    xn_ref[...] = xn
    x_hi = xn.astype(BF16)
    x_lo = (xn - x_hi.astype(F32)).astype(BF16)
    w = wr_ref[...]
    w_hi = w.astype(BF16)
    w_lo = (w - w_hi.astype(F32)).astype(BF16)
    lg_ref[...] = (jnp.dot(x_hi, w_hi, preferred_element_type=F32) + jnp.dot(x_hi, w_lo, preferred_element_type=F32)
                   + jnp.dot(x_lo, w_hi, preferred_element_type=F32))


def norm_router(h, g, w_router):
    m, d = h.shape
    e = w_router.shape[1]
    wr = jnp.pad(w_router.astype(F32), ((0, 0), (0, LANES - e)))
    tm = _pick(m, (512, 256, 128, 64, 32, 16, 8))
    vm = 2 * (2 * _nbytes((tm, d), F32) + _nbytes((d, LANES), F32)) + 6 * _nbytes((tm, d), F32)
    xn, lg = pl.pallas_call(
        functools.partial(_norm_router_kernel, eps=RMS_EPS),
        out_shape=(jax.ShapeDtypeStruct((m, d), F32), jax.ShapeDtypeStruct((m, LANES), F32)),
        grid=(m // tm,),
        in_specs=[pl.BlockSpec((tm, d), lambda i: (i, 0)),
                  pl.BlockSpec((1, d), lambda i: (0, 0)),
                  pl.BlockSpec((d, LANES), lambda i: (0, 0))],
        out_specs=(pl.BlockSpec((tm, d), lambda i: (i, 0)), pl.BlockSpec((tm, LANES), lambda i: (i, 0))),
        compiler_params=_params(("parallel",), vm),
        name="norm_router",
    )(h, g.reshape(1, d).astype(F32), wr)
    return xn, lg[:, :e]


def _row_copy(src_hbm, src_row, dst, dst_row, sem):
    return pltpu.make_async_copy(src_hbm.at[pl.ds(src_row, 1)], dst.at[pl.ds(dst_row, 1)], sem)


def _gather_rows_kernel(idx_ref, x_hbm, o_ref, sem, *, rows):
    base = pl.program_id(0) * rows

    def start(r, carry):
        _row_copy(x_hbm, idx_ref[base + r], o_ref, r, sem).start()
        return carry

    lax.fori_loop(0, rows, start, 0)

    def wait(r, carry):
        _row_copy(x_hbm, 0, o_ref, r, sem).wait()
        return carry

    lax.fori_loop(0, rows, wait, 0)


def gather_rows(x, idx):
    m = idx.shape[0]
    d = x.shape[1]
    rows = _pick(m, (256, 128, 64, 32, 16, 8))
    return pl.pallas_call(
        functools.partial(_gather_rows_kernel, rows=rows),
        out_shape=jax.ShapeDtypeStruct((m, d), x.dtype),
        grid_spec=pltpu.PrefetchScalarGridSpec(
            num_scalar_prefetch=1, grid=(m // rows,),
            in_specs=[pl.BlockSpec(memory_space=pl.ANY)],
            out_specs=pl.BlockSpec((rows, d), lambda t, idx_ref: (t, 0)),
            scratch_shapes=[pltpu.SemaphoreType.DMA(())]),
        compiler_params=_params(("arbitrary",), 4 * _nbytes((rows, d), x.dtype)),
        name="moe_gather",
    )(idx, x)


def _moe_up_kernel(te_ref, tv_ref, x_ref, wg_ref, wu_ref, gate_ref, o_ref):
    t = pl.program_id(0)

    @pl.when(tv_ref[t] == 1)
    def _():
        x = x_ref[...].astype(BF16)
        g = jnp.dot(x, wg_ref[...], preferred_element_type=F32)
        u = jnp.dot(x, wu_ref[...], preferred_element_type=F32)
        o_ref[...] = (gate_ref[...] * (g * jax.nn.sigmoid(g) * u)).astype(o_ref.dtype)

    @pl.when(tv_ref[t] == 0)
    def _():
        o_ref[...] = jnp.zeros_like(o_ref)


def _moe_down_kernel(te_ref, tv_ref, a_ref, w_ref, o_ref):
    t = pl.program_id(0)

    @pl.when(tv_ref[t] == 1)
    def _():
        o_ref[...] = jnp.dot(a_ref[...], w_ref[...], preferred_element_type=F32)

    @pl.when(tv_ref[t] == 0)
    def _():
        o_ref[...] = jnp.zeros_like(o_ref)


def _moe_combine_kernel(s1_ref, s2_ref, h_ref, g_ref, d_hbm, o_ref, on_ref, buf, sem, *, tm):
    base = pl.program_id(0) * tm

    def start(r, carry):
        _row_copy(d_hbm, s1_ref[base + r], buf.at[0], r, sem).start()
        _row_copy(d_hbm, s2_ref[base + r], buf.at[1], r, sem).start()
        return carry

    lax.fori_loop(0, tm, start, 0)

    def wait(r, carry):
        _row_copy(d_hbm, 0, buf.at[0], r, sem).wait()
        _row_copy(d_hbm, 0, buf.at[1], r, sem).wait()
        return carry

    lax.fori_loop(0, tm, wait, 0)
    o = h_ref[...] + buf[0] + buf[1]
    o_ref[...] = o
    ms = jnp.mean(o * o, axis=-1, keepdims=True)
    on_ref[...] = (o * lax.rsqrt(ms + RMS_EPS) * g_ref[...]).astype(on_ref.dtype)


def _route(logits, tile):
    s, ne = logits.shape
    top_val, top_idx = lax.top_k(logits, TOP_K)
    top_w = jax.nn.softmax(top_val, axis=-1)
    e_flat = top_idx.reshape(-1).astype(jnp.int32)
    onehot = (e_flat[:, None] == jnp.arange(ne, dtype=jnp.int32)[None, :]).astype(jnp.int32)
    csum = jnp.cumsum(onehot, axis=0)
    rank = jnp.sum(csum * onehot, axis=1) - 1
    counts = csum[-1]
    padded = (counts + tile - 1) // tile * tile
    ends = jnp.cumsum(padded)
    slot = (ends - padded)[e_flat] + rank
    p_rows = s * TOP_K + ne * tile
    tok = (jnp.arange(s * TOP_K, dtype=jnp.int32) // TOP_K).astype(F32)
    placed = jnp.zeros((p_rows, 2), F32).at[slot].set(jnp.stack([tok, top_w.reshape(-1)], axis=1))
    src = placed[:, 0].astype(jnp.int32)
    gate = placed[:, 1]
    tstart = jnp.arange(p_rows // tile, dtype=jnp.int32) * tile
    tile_e = jnp.minimum(jnp.sum((tstart[:, None] >= ends[None, :]).astype(jnp.int32), axis=1), ne - 1)
    tile_valid = (tstart < ends[-1]).astype(jnp.int32)
    return src, gate.reshape(p_rows, 1), tile_e, tile_valid, slot.reshape(s, TOP_K)


def _rw_prep_kernel(yr_ref, yk_ref, yv_ref, yl_ref, pr_ref, pk_ref, pv_ref, pl_ref,
                    mur_ref, muk_ref, muv_ref, mul_ref, w2_ref, a2_ref, g2_ref, w0_ref, a0_ref,
                    r_ref, k_ref, v_ref, lw_ref, ai_ref, g_ref):
    first = pl.program_id(0) == 0

    def shifted(x_ref, p_ref, mu_ref):
        x = x_ref[...].astype(F32)
        prev = jnp.where(first, 0.0, p_ref[...].astype(F32)[7:8, :])
        row = lax.broadcasted_iota(jnp.int32, x.shape, 0)
        xp = jnp.where(row == 0, prev, pltpu.roll(x, 1, axis=0))
        return x + (xp - x) * mu_ref[...]

    r_ref[...] = shifted(yr_ref, pr_ref, mur_ref)
    k_ref[...] = shifted(yk_ref, pk_ref, muk_ref)
    v_ref[...] = shifted(yv_ref, pv_ref, muv_ref)
    lo = shifted(yl_ref, pl_ref, mul_ref)
    lane = lax.broadcasted_iota(jnp.int32, lo.shape, 1)
    act = jnp.where(lane < DECAY_LORA, jnp.tanh(lo),
                    jnp.where(lane < DECAY_LORA + AAA_LORA, lo,
                              jnp.where(lane < DECAY_LORA + AAA_LORA + GATE_LORA, jax.nn.sigmoid(lo), 0.0)))
    act = act.astype(BF16)
    w_pre = w0_ref[...] + jnp.dot(act, w2_ref[...], preferred_element_type=F32)
    x = -w_pre
    softplus = jnp.maximum(x, 0.0) + jnp.log(1.0 + jnp.exp(-jnp.abs(x)))
    lw_ref[...] = -jnp.exp(-softplus - 0.5)
    ai_ref[...] = jax.nn.sigmoid(a0_ref[...] + jnp.dot(act, a2_ref[...], preferred_element_type=F32))
    g_ref[...] = jnp.dot(act, g2_ref[...], preferred_element_type=F32)


def _rwkv_group(y, col_lo, mu, w0, w2, a0, a2, g2, k_k, k_a, r_k, lnx_w, lnx_b):
    s = y.shape[0]
    dim = RW_DIM
    lora = DECAY_LORA + AAA_LORA + GATE_LORA
    lwid = -(-lora // LANES) * LANES
    assert col_lo % lwid == 0 and dim % LANES == 0
    tm = _pick(s, (128, 64, 32, 16, 8))
    pb = tm // 8
    mu_rkv = mu[:3 * dim].reshape(1, 3 * dim)
    mu_lo = jnp.pad(mu[3 * dim:], (0, lwid - lora)).reshape(1, lwid)

    def pad_rows(w, start):
        return jnp.pad(w, ((start, lwid - start - w.shape[0]), (0, 0))).astype(BF16)

    y_spec = lambda c: pl.BlockSpec((tm, dim), lambda i: (i, c))
    p_spec = lambda c: pl.BlockSpec((8, dim), lambda i: (jnp.maximum(i * pb - 1, 0), c))
    mu_spec = lambda c: pl.BlockSpec((1, dim), lambda i: (0, c))
    cl = col_lo // lwid
    w_spec = pl.BlockSpec((lwid, dim), lambda i: (0, 0))
    row_spec = pl.BlockSpec((1, dim), lambda i: (0, 0))
    out_spec = pl.BlockSpec((tm, dim), lambda i: (i, 0))
    out_sds = jax.ShapeDtypeStruct((s, dim), F32)
    vm = 2 * (3 * _nbytes((tm, dim), BF16) + 6 * _nbytes((tm, dim), F32) + 3 * _nbytes((lwid, dim), BF16))
    vm += 6 * _nbytes((tm, dim), F32)
    r, k, v, lw, ai, g = pl.pallas_call(
        _rw_prep_kernel,
        out_shape=(out_sds,) * 6,
        grid=(s // tm,),
        in_specs=[y_spec(0), y_spec(1), y_spec(2), pl.BlockSpec((tm, lwid), lambda i: (i, cl)),
                  p_spec(0), p_spec(1), p_spec(2),
                  pl.BlockSpec((8, lwid), lambda i: (jnp.maximum(i * pb - 1, 0), cl)),
                  mu_spec(0), mu_spec(1), mu_spec(2), pl.BlockSpec((1, lwid), lambda i: (0, 0)),
                  w_spec, w_spec, w_spec, row_spec, row_spec],
        out_specs=(out_spec,) * 6,
        compiler_params=_params(("parallel",), vm),
        name="rwkv_prep",
    )(y, y, y, y, y, y, y, y, mu_rkv, mu_rkv, mu_rkv, mu_lo,
      pad_rows(w2, 0), pad_rows(a2, DECAY_LORA), pad_rows(g2, DECAY_LORA + AAA_LORA),
      w0.reshape(1, dim), a0.reshape(1, dim))
    return rwkv_mix(r, k, v, ai, lw, g, k_k, k_a, r_k.reshape(-1), lnx_w, lnx_b)


def _rope(x, cos, sin):
    half = x.shape[-1] // 2
    x1, x2 = x[..., :half], x[..., half:]
    return jnp.concatenate([x1 * cos - x2 * sin, x2 * cos + x1 * sin], axis=-1)


def _mla_group(y, col_q, col_kv, col_pe, cos, sin, q_norm, w_uq, kv_norm, w_ukv):
    s = y.shape[0]
    nh = MLA_HEADS
    cqn = rmsnorm(y, q_norm, BF16, width=Q_LORA, col_block=col_q // Q_LORA)
    ckvn = rmsnorm(y, kv_norm, BF16, width=KV_LORA, col_block=col_kv // KV_LORA)
    assert QK_NOPE == LANES and V_HEAD == LANES and QK_ROPE <= LANES
    half = QK_ROPE // 2
    scale = (QK_NOPE + QK_ROPE) ** -0.5 * 1.4426950408889634
    wq = w_uq.reshape(Q_LORA, nh, QK_NOPE + QK_ROPE)
    w_nope = wq[:, :, :QK_NOPE].reshape(Q_LORA, nh * LANES).astype(BF16)
    w_pe = wq[:, :, QK_NOPE:]
    w_rot = jnp.concatenate([-w_pe[..., half:], w_pe[..., :half]], axis=-1)
    lane_pad = ((0, 0), (0, 0), (0, LANES - QK_ROPE))
    w_pe = jnp.pad(w_pe, lane_pad).reshape(Q_LORA, nh * LANES).astype(BF16)
    w_rot = jnp.pad(w_rot, lane_pad).reshape(Q_LORA, nh * LANES).astype(BF16)
    tm = _pick(s, (1024, 512, 256, 128, 64, 32, 16, 8))
    tn = _pick(nh * LANES, (512, 256, 128))
    table_pad = ((0, 0), (0, LANES - QK_ROPE))
    cos_t = jnp.tile(jnp.pad(jnp.concatenate([cos, cos], axis=-1), table_pad), (1, tn // LANES))
    sin_t = jnp.tile(jnp.pad(jnp.concatenate([sin, sin], axis=-1), table_pad), (1, tn // LANES))
    q_nope = matmul(cqn, [w_nope], lambda accs, extras: accs[0] * scale, BF16, tm=tm, tn=tn, name="mla_q_nope_up")
    q_pe = matmul(cqn, [w_pe, w_rot], lambda accs, extras: (accs[0] * extras[0] + accs[1] * extras[1]) * scale,
                  BF16, tm=tm, tn=tn,
                  extras=[(cos_t, (tm, tn), lambda i, jn: (i, 0)), (sin_t, (tm, tn), lambda i, jn: (i, 0))],
                  name="mla_q_rope_up")
    w_kv = jnp.transpose(w_ukv.reshape(KV_LORA, nh, 2, LANES), (0, 2, 1, 3)).reshape(KV_LORA, 2 * nh * LANES)
    kv = matmul(ckvn, [w_kv.astype(BF16)], _epi_id, BF16, tm=tm, tn=tn, name="mla_kv_up")
    k_pe = _rope(y[:, col_pe:col_pe + QK_ROPE].astype(F32), cos, sin)
    k_pe = jnp.pad(k_pe, table_pad).astype(BF16)
    return flash_attention(q_nope, q_pe, kv, k_pe, nh)


def _even_layer(h, hn, j, cos, sin, prm):
    w_in = prm["ev_in_proj"][j]
    rw_end = 3 * RW_DIM
    lora_w = DECAY_LORA + AAA_LORA + GATE_LORA
    rw_proj = rw_end + lora_w
    w_perm = jnp.concatenate([
        w_in[:, :rw_end],
        w_in[:, rw_proj:rw_proj + Q_LORA + KV_LORA],
        w_in[:, rw_end:rw_proj],
        w_in[:, rw_proj + Q_LORA + KV_LORA:],
    ], axis=1).astype(BF16)
    y = matmul(hn, [w_perm], _epi_id, BF16, tm=1024, tn=512, name="even_in_proj")
    col_q = rw_end
    col_kv = rw_end + Q_LORA
    col_lo = rw_end + Q_LORA + KV_LORA
    col_pe = col_lo + lora_w
    mu = prm["rw_mu"][j]
    o_rw = _rwkv_group(y, col_lo, mu, prm["rw_w0"][j], prm["rw_w2"][j], prm["rw_a0"][j], prm["rw_a2"][j],
                       prm["rw_g2"][j], prm["rw_k_k"][j], prm["rw_k_a"][j], prm["rw_r_k"][j],
                       prm["rw_lnx_w"][j], prm["rw_lnx_b"][j])
    o_mla = _mla_group(y, col_q, col_kv, col_pe, cos, sin, prm["mla_q_norm"][j], prm["mla_w_uq"][j],
                       prm["mla_kv_norm"][j], prm["mla_w_ukv"][j])
    mix = jnp.concatenate([o_rw, o_mla], axis=-1)
    tm, tn = 1024, 512
    h = matmul(mix, [prm["ev_out_proj"][j].astype(BF16)], _epi_residual, F32, tm=tm, tn=tn,
               extras=[_res_extra(h, _pick(h.shape[0], (tm, 512, 256, 128, 64, 32, 16, 8)),
                                  _pick(h.shape[1], (tn, 512, 256, 128)), 1)], name="even_out_proj")
    return h


def _dense_ffn(h, hn, wg, wu, wd):
    d, f = wg.shape
    fp = -(-f // 1024) * 1024 if f > 1024 else f
    pad = fp - f
    wg = jnp.pad(wg.astype(BF16), ((0, 0), (0, pad)))
    wu = jnp.pad(wu.astype(BF16), ((0, 0), (0, pad)))
    wd = jnp.pad(wd.astype(BF16), ((0, pad), (0, 0)))
    mid = matmul(hn, [wg, wu], _epi_swiglu, BF16, tm=1024, tn=512, name="ffn_up")
    tm = _pick(h.shape[0], (1024, 512, 256, 128, 64, 32, 16, 8))
    tn = _pick(h.shape[1], (1024, 512, 256, 128))
    tk = _pick(fp, (2816, 2048, 1024, 512, 256, 128))
    nk = fp // tk
    return matmul(mid, [wd], _epi_residual, F32, tm=tm, tn=tn, tk=tk,
                  extras=[_res_extra(h, tm, tn, nk)], name="ffn_down")


def _moe(h, g_norm, w_router, wg, wu, wd, g_next):
    ne, d, f = wg.shape
    s = h.shape[0]
    xn, logits = norm_router(h, g_norm, w_router)
    tile = _pick(s, (MOE_TILE, 256, 128, 64, 32, 16, 8))
    src, gate, tile_e, tile_valid, slots = _route(logits, tile)
    p_rows = src.shape[0]
    n_tiles = p_rows // tile
    xs = gather_rows(xn, src)
    tn = _pick(f, (512, 256, 128))
    vm = 2 * (_nbytes((tile, d), F32) + 2 * _nbytes((d, tn), BF16) + _nbytes((tile, tn), BF16))
    vm += _nbytes((tile, d), BF16) + 4 * _nbytes((tile, tn), F32)
    w_spec = pl.BlockSpec((None, d, tn), lambda t, jn, te, tv: (te[t], 0, jn))
    mid = pl.pallas_call(
        _moe_up_kernel,
        out_shape=jax.ShapeDtypeStruct((p_rows, f), BF16),
        grid_spec=pltpu.PrefetchScalarGridSpec(
            num_scalar_prefetch=2, grid=(n_tiles, f // tn),
            in_specs=[pl.BlockSpec((tile, d), lambda t, jn, te, tv: (t, 0)), w_spec, w_spec,
                      pl.BlockSpec((tile, 1), lambda t, jn, te, tv: (t, 0))],
            out_specs=pl.BlockSpec((tile, tn), lambda t, jn, te, tv: (t, jn))),
        compiler_params=_params(("parallel", "parallel"), vm),
        name="moe_up",
    )(tile_e, tile_valid, xs, wg.astype(BF16), wu.astype(BF16), gate)
    tn2 = _pick(d, (1024, 512, 256, 128))
    vm = 2 * (_nbytes((tile, f), BF16) + _nbytes((f, tn2), BF16) + _nbytes((tile, tn2), F32)) + _nbytes((tile, tn2), F32)
    dn = pl.pallas_call(
        _moe_down_kernel,
        out_shape=jax.ShapeDtypeStruct((p_rows, d), F32),
        grid_spec=pltpu.PrefetchScalarGridSpec(
            num_scalar_prefetch=2, grid=(n_tiles, d // tn2),
            in_specs=[pl.BlockSpec((tile, f), lambda t, jn, te, tv: (t, 0)),
                      pl.BlockSpec((None, f, tn2), lambda t, jn, te, tv: (te[t], 0, jn))],
            out_specs=pl.BlockSpec((tile, tn2), lambda t, jn, te, tv: (t, jn))),
        compiler_params=_params(("parallel", "parallel"), vm),
        name="moe_down",
    )(tile_e, tile_valid, mid, wd.astype(BF16))
    tm = _pick(s, (256, 128, 64, 32, 16, 8))
    vm = 4 * _nbytes((tm, d), F32) + 2 * _nbytes((tm, d), F32) + 2 * _nbytes((tm, d), F32)
    return pl.pallas_call(
        functools.partial(_moe_combine_kernel, tm=tm),
        out_shape=(jax.ShapeDtypeStruct((s, d), F32), jax.ShapeDtypeStruct((s, d), BF16)),
        grid_spec=pltpu.PrefetchScalarGridSpec(
            num_scalar_prefetch=2, grid=(s // tm,),
            in_specs=[pl.BlockSpec((tm, d), lambda i, s1, s2: (i, 0)), pl.BlockSpec((1, d), lambda i, s1, s2: (0, 0)),
                      pl.BlockSpec(memory_space=pl.ANY)],
            out_specs=(pl.BlockSpec((tm, d), lambda i, s1, s2: (i, 0)), pl.BlockSpec((tm, d), lambda i, s1, s2: (i, 0))),
            scratch_shapes=[pltpu.VMEM((2, tm, d), F32), pltpu.SemaphoreType.DMA(())]),
        compiler_params=_params(("arbitrary",), vm),
        name="moe_combine",
    )(slots[:, 0], slots[:, 1], h, g_next.reshape(1, d).astype(F32), dn)


def _odd_layer(h, hn, j, prm):
    z = matmul(hn, [prm["sg_in_proj"][j].astype(BF16)], _epi_gelu, BF16, tm=1024, tn=512, name="sg_in_proj")
    d = h.shape[1]
    vn = layernorm(z, prm["sg_ln_w"][j], prm["sg_ln_b"][j], BF16, width=d, col_block=1)
    gated = spatial_gate(z, vn, prm["sg_w_spatial"][j], prm["sg_b_spatial"][j])
    tm = _pick(h.shape[0], (1024, 512, 256, 128, 64, 32, 16, 8))
    tn = _pick(d, (512, 256, 128))
    return matmul(gated, [prm["sg_out_proj"][j].astype(BF16)], _epi_residual, F32, tm=tm, tn=tn,
                  extras=[_res_extra(h, tm, tn, 1)], name="sg_out_proj")


def kernel(x, p, positions, norm_mix, norm_ffn, norm_ple, final_norm, ev_in_proj, rw_mu, rw_w0, rw_w2, rw_a0, rw_a2, rw_g2, rw_k_k, rw_k_a, rw_r_k, rw_lnx_w, rw_lnx_b, mla_q_norm, mla_w_uq, mla_kv_norm, mla_w_ukv, ev_out_proj, ffn_w_gate, ffn_w_up, ffn_w_down, sg_in_proj, sg_ln_w, sg_ln_b, sg_w_spatial, sg_b_spatial, sg_out_proj, moe_router, moe_w_gate, moe_w_up, moe_w_down, ple_proj, ple_gate):
    prm = dict(ev_in_proj=ev_in_proj, rw_mu=rw_mu, rw_w0=rw_w0, rw_w2=rw_w2, rw_a0=rw_a0, rw_a2=rw_a2, rw_g2=rw_g2,
               rw_k_k=rw_k_k, rw_k_a=rw_k_a, rw_r_k=rw_r_k, rw_lnx_w=rw_lnx_w, rw_lnx_b=rw_lnx_b,
               mla_q_norm=mla_q_norm, mla_w_uq=mla_w_uq, mla_kv_norm=mla_kv_norm, mla_w_ukv=mla_w_ukv,
               ev_out_proj=ev_out_proj, sg_in_proj=sg_in_proj, sg_ln_w=sg_ln_w, sg_ln_b=sg_ln_b,
               sg_w_spatial=sg_w_spatial, sg_b_spatial=sg_b_spatial, sg_out_proj=sg_out_proj)
    bsz, s, d = x.shape
    assert bsz == 1
    inv = 1.0 / (ROPE_THETA ** (jnp.arange(0, QK_ROPE, 2, dtype=F32) / QK_ROPE))
    ang = positions[0].astype(F32)[:, None] * inv
    cos, sin = jnp.cos(ang), jnp.sin(ang)
    h = x[0]
    for i in range(DEPTH):
        j = i // 2
        if i % 2 == 0:
            hn = rmsnorm(h, norm_mix[i], BF16)
            h = _even_layer(h, hn, j, cos, sin, prm)
            hn = rmsnorm(h, norm_ffn[i], BF16)
            h = _dense_ffn(h, hn, ffn_w_gate[j], ffn_w_up[j], ffn_w_down[j])
        else:
            hn = rmsnorm(h, norm_mix[i], BF16)
            h = _odd_layer(h, hn, j, prm)
            h, hn = _moe(h, norm_ffn[i], moe_router[j], moe_w_gate[j], moe_w_up[j], moe_w_down[j], norm_ple[i])
        if i % 2 == 0:
            hn = rmsnorm(h, norm_ple[i], BF16)
        h = ple(hn, ple_gate[i].astype(BF16), p[i, 0].astype(BF16), ple_proj[i].astype(BF16), h)
    return rmsnorm(h, final_norm, F32)[None]
```

```python
import functools

import jax
import jax.numpy as jnp
from jax import lax
from jax.experimental import pallas as pl
from jax.experimental.pallas import tpu as pltpu

D_MODEL = 4096
DEPTH = 4
RW_HEADS = 32
RW_HEAD = 64
RW_DIM = RW_HEADS * RW_HEAD
DECAY_LORA = 96
AAA_LORA = 96
GATE_LORA = 256
GN_EPS = 64e-5
MLA_HEADS = 16
Q_LORA = 1536
KV_LORA = 512
QK_NOPE = 128
QK_ROPE = 64
V_HEAD = 128
ROPE_THETA = 10000.0
CHUNK = 128
SG_GROUPS = 32
LN_EPS = 1e-5
D_FF = 11008
N_EXPERTS = 8
TOP_K = 2
D_FF_EXPERT = 1536
PLE_DIM = 256
RMS_EPS = 1e-6

V7X_VMEM_BYTES = 64 * 1024 * 1024
LANES = 128
RW_CHUNK = 64
RW_HEADS_PER_STEP = 8
RW_CHUNKS_PER_STEP = 2
FLASH_T = 1024
MOE_TILE = 512

F32 = jnp.float32
BF16 = jnp.bfloat16


def _pick(dim, candidates):
    for c in candidates:
        if dim % c == 0:
            return c
    return dim


def _params(sem, vmem_bytes, flags=None):
    limit = int(min(V7X_VMEM_BYTES - 6 * 1024 * 1024, max(vmem_bytes * 5 // 4 + (2 << 20), 16 << 20)))
    return pltpu.CompilerParams(dimension_semantics=sem, vmem_limit_bytes=limit, flags=flags)


def _nbytes(shape, dtype):
    n = 1
    for s in shape:
        n *= s
    return n * jnp.dtype(dtype).itemsize


def _rmsnorm_kernel(x_ref, g_ref, o_ref, *, eps):
    x = x_ref[...].astype(F32)
    ms = jnp.mean(x * x, axis=-1, keepdims=True)
    o_ref[...] = (x * lax.rsqrt(ms + eps) * g_ref[...]).astype(o_ref.dtype)


def rmsnorm(x, g, out_dtype, *, width=None, col_block=0):
    m = x.shape[0]
    d = x.shape[1] if width is None else width
    tm = _pick(m, (512, 256, 128, 64, 32, 16, 8))
    vm = 2 * (_nbytes((tm, d), x.dtype) + _nbytes((tm, d), out_dtype)) + 4 * _nbytes((tm, d), F32)
    return pl.pallas_call(
        functools.partial(_rmsnorm_kernel, eps=RMS_EPS),
        out_shape=jax.ShapeDtypeStruct((m, d), out_dtype),
        grid=(m // tm,),
        in_specs=[pl.BlockSpec((tm, d), lambda i: (i, col_block)),
                  pl.BlockSpec((1, d), lambda i: (0, 0))],
        out_specs=pl.BlockSpec((tm, d), lambda i: (i, 0)),
        compiler_params=_params(("parallel",), vm),
        name="rmsnorm",
    )(x, g.reshape(1, d).astype(F32))


def _layernorm_kernel(x_ref, w_ref, b_ref, o_ref, *, eps):
    x = x_ref[...].astype(F32)
    mean = jnp.mean(x, axis=-1, keepdims=True)
    var = jnp.mean(jnp.square(x - mean), axis=-1, keepdims=True)
    o_ref[...] = ((x - mean) * lax.rsqrt(var + eps) * w_ref[...] + b_ref[...]).astype(o_ref.dtype)


def layernorm(x, w, b, out_dtype, *, width, col_block):
    m = x.shape[0]
    tm = _pick(m, (512, 256, 128, 64, 32, 16, 8))
    vm = 2 * (_nbytes((tm, width), x.dtype) + _nbytes((tm, width), out_dtype)) + 4 * _nbytes((tm, width), F32)
    return pl.pallas_call(
        functools.partial(_layernorm_kernel, eps=LN_EPS),
        out_shape=jax.ShapeDtypeStruct((m, width), out_dtype),
        grid=(m // tm,),
        in_specs=[pl.BlockSpec((tm, width), lambda i: (i, col_block)),
                  pl.BlockSpec((1, width), lambda i: (0, 0)),
                  pl.BlockSpec((1, width), lambda i: (0, 0))],
        out_specs=pl.BlockSpec((tm, width), lambda i: (i, 0)),
        compiler_params=_params(("parallel",), vm),
        name="layernorm",
    )(x, w.reshape(1, width).astype(F32), b.reshape(1, width).astype(F32))


def _mm_kernel(*refs, nk, n_w, n_extra, epi):
    a_ref = refs[0]
    w_refs = refs[1:1 + n_w]
    extra_refs = refs[1 + n_w:1 + n_w + n_extra]
    o_ref = refs[1 + n_w + n_extra]
    acc_refs = refs[2 + n_w + n_extra:]
    a = a_ref[...]
    if nk == 1:
        accs = [jnp.dot(a, w[...], preferred_element_type=F32) for w in w_refs]
        o_ref[...] = epi(accs, [e[...] for e in extra_refs]).astype(o_ref.dtype)
        return
    k = pl.program_id(2)

    @pl.when(k == 0)
    def _():
        for acc in acc_refs:
            acc[...] = jnp.zeros_like(acc)

    for acc, w in zip(acc_refs, w_refs):
        acc[...] += jnp.dot(a, w[...], preferred_element_type=F32)

    @pl.when(k == nk - 1)
    def _():
        o_ref[...] = epi([acc[...] for acc in acc_refs], [e[...] for e in extra_refs]).astype(o_ref.dtype)


def matmul(a, ws, epi, out_dtype, *, extras=(), tm=1024, tn=512, tk=None, a_col_block=0, a_width=None,
           w_specs=None, n_out=None, name="matmul"):
    m = a.shape[0]
    kdim = a.shape[1] if a_width is None else a_width
    n = ws[0].shape[-1] if n_out is None else n_out
    tm = _pick(m, (tm, 512, 256, 128, 64, 32, 16, 8))
    tn = _pick(n, (tn, 512, 256, 128))
    tk = kdim if tk is None else _pick(kdim, (tk,))
    nk = kdim // tk
    kb0 = a_col_block * nk
    if nk == 1:
        grid = (m // tm, n // tn)
        a_spec = pl.BlockSpec((tm, tk), lambda i, j: (i, kb0))
        w_spec_list = w_specs or [pl.BlockSpec((tk, tn), lambda i, j: (0, j)) for _ in ws]
        o_spec = pl.BlockSpec((tm, tn), lambda i, j: (i, j))
        sem = ("parallel", "parallel")
        scratch = []
    else:
        grid = (m // tm, n // tn, nk)
        a_spec = pl.BlockSpec((tm, tk), lambda i, j, k: (i, kb0 + k))
        w_spec_list = w_specs or [pl.BlockSpec((tk, tn), lambda i, j, k: (k, j)) for _ in ws]
        o_spec = pl.BlockSpec((tm, tn), lambda i, j, k: (i, j))
        sem = ("parallel", "parallel", "arbitrary")
        scratch = [pltpu.VMEM((tm, tn), F32) for _ in ws]
    extra_specs = [pl.BlockSpec(bs, im) for (_, bs, im) in extras]
    vm = 2 * _nbytes((tm, tk), a.dtype) + 2 * len(ws) * _nbytes((tk, tn), ws[0].dtype)
    vm += 2 * _nbytes((tm, tn), out_dtype) + (len(ws) + 2) * _nbytes((tm, tn), F32)
    for (arr, bs, _) in extras:
        vm += 2 * _nbytes([1 if b is None else b for b in bs], arr.dtype)
    return pl.pallas_call(
        functools.partial(_mm_kernel, nk=nk, n_w=len(ws), n_extra=len(extras), epi=epi),
        out_shape=jax.ShapeDtypeStruct((m, n), out_dtype),
        grid=grid,
        in_specs=[a_spec] + list(w_spec_list) + extra_specs,
        out_specs=o_spec,
        scratch_shapes=scratch,
        compiler_params=_params(sem, vm),
        name=name,
    )(a, *ws, *[e[0] for e in extras])


def _epi_id(accs, extras):
    return accs[0]


def _epi_residual(accs, extras):
    return extras[0] + accs[0]


def _epi_swiglu(accs, extras):
    g, u = accs
    return g * jax.nn.sigmoid(g) * u


def _epi_gelu(accs, extras):
    z = accs[0]
    return 0.5 * z * (1.0 + lax.erf(z * (2.0 ** -0.5)))


def _res_extra(h, tm, tn, nk):
    if nk == 1:
        return (h, (tm, tn), lambda i, j: (i, j))
    return (h, (tm, tn), lambda i, j, k: (i, j))


def _mix_out_kernel(a1_ref, a2_ref, w1_ref, w2_ref, h_ref, o_ref):
    o_ref[...] = (h_ref[...] + jnp.dot(a1_ref[...], w1_ref[...], preferred_element_type=F32)
                  + jnp.dot(a2_ref[...], w2_ref[...], preferred_element_type=F32))


def mix_out_proj(a1, a2, w, h):
    m, kh = a1.shape
    n = w.shape[1]
    assert a2.shape == a1.shape and w.shape[0] == 2 * kh
    tm = _pick(m, (1024, 512, 256, 128, 64, 32, 16, 8))
    tn = _pick(n, (512, 256, 128))
    vm = 2 * (2 * _nbytes((tm, kh), BF16) + 2 * _nbytes((kh, tn), BF16) + 2 * _nbytes((tm, tn), F32))
    vm += 3 * _nbytes((tm, tn), F32)
    return pl.pallas_call(
        _mix_out_kernel,
        out_shape=jax.ShapeDtypeStruct((m, n), F32),
        grid=(m // tm, n // tn),
        in_specs=[pl.BlockSpec((tm, kh), lambda i, j: (i, 0)),
                  pl.BlockSpec((tm, kh), lambda i, j: (i, 0)),
                  pl.BlockSpec((kh, tn), lambda i, j: (0, j)),
                  pl.BlockSpec((kh, tn), lambda i, j: (1, j)),
                  pl.BlockSpec((tm, tn), lambda i, j: (i, j))],
        out_specs=pl.BlockSpec((tm, tn), lambda i, j: (i, j)),
        compiler_params=_params(("parallel", "parallel"), vm),
        name="even_out_proj",
    )(a1, a2, w, w, h)


def _ple_kernel(hn_ref, wg_ref, p_ref, wp_ref, h_ref, o_ref):
    gate = jax.nn.sigmoid(jnp.dot(hn_ref[...], wg_ref[...], preferred_element_type=F32))
    proj = jnp.dot(p_ref[...], wp_ref[...], preferred_element_type=F32)
    o_ref[...] = h_ref[...] + gate * proj


def ple(hn, w_gate, p, w_proj, h):
    m, d = hn.shape
    n = w_gate.shape[1]
    pd = p.shape[1]
    tm = _pick(m, (1024, 512, 256, 128, 64, 32, 16, 8))
    tn = _pick(n, (512, 256, 128))
    vm = 2 * (_nbytes((tm, d), BF16) + _nbytes((d, tn), BF16) + _nbytes((tm, pd), BF16) + _nbytes((pd, tn), BF16))
    vm += 6 * _nbytes((tm, tn), F32)
    return pl.pallas_call(
        _ple_kernel,
        out_shape=jax.ShapeDtypeStruct((m, n), F32),
        grid=(m // tm, n // tn),
        in_specs=[pl.BlockSpec((tm, d), lambda i, j: (i, 0)),
                  pl.BlockSpec((d, tn), lambda i, j: (0, j)),
                  pl.BlockSpec((tm, pd), lambda i, j: (i, 0)),
                  pl.BlockSpec((pd, tn), lambda i, j: (0, j)),
                  pl.BlockSpec((tm, tn), lambda i, j: (i, j))],
        out_specs=pl.BlockSpec((tm, tn), lambda i, j: (i, j)),
        compiler_params=_params(("parallel", "parallel"), vm),
        name="ple",
    )(hn, w_gate, p, w_proj, h)


def _bdot(a, b):
    return jnp.dot(a.astype(BF16), b.astype(BF16), preferred_element_type=F32)


def _bdot_nt(a, b):
    return lax.dot_general(a.astype(BF16), b.astype(BF16), (((1,), (1,)), ((), ())),
                           preferred_element_type=F32)


def _bdot_tn(a, b):
    return lax.dot_general(a.astype(BF16), b.astype(BF16), (((0,), (0,)), ((), ())),
                           preferred_element_type=F32)


def _split3(x):
    hi = x.astype(BF16)
    r1 = x - hi.astype(F32)
    mid = r1.astype(BF16)
    lo = (r1 - mid.astype(F32)).astype(BF16)
    return hi, mid, lo


def _bmm(a, b):
    return jnp.einsum("bij,bjk->bik", a.astype(BF16), b.astype(BF16), preferred_element_type=F32)


def _bmm_nt(a, b):
    return jnp.einsum("bik,bjk->bij", a.astype(BF16), b.astype(BF16), preferred_element_type=F32)


def _bmm_tn(a, b):
    return jnp.einsum("bti,btj->bij", a.astype(BF16), b.astype(BF16), preferred_element_type=F32)


def _rwkv_a_kernel(r_ref, k_ref, v_ref, ai_ref, lw_ref, kk_ref, ka_ref, mt_ref, nt_ref, q_ref, o0_ref, *, hb, ncb):
    t = RW_CHUNK
    n = RW_HEAD
    nb = hb * ncb
    row = lax.broadcasted_iota(jnp.int32, (t, t), 0)
    col = lax.broadcasted_iota(jnp.int32, (t, t), 1)
    incl = col <= row
    strict = col < row
    tri = jnp.broadcast_to(incl.astype(BF16), (nb, t, t))
    eye = (row == col).astype(F32)
    drow = lax.broadcasted_iota(jnp.int32, (n, n), 0)
    dcol = lax.broadcasted_iota(jnp.int32, (n, n), 1)

    def units(ref):
        return jnp.stack([ref[c * t:(c + 1) * t, h * n:(h + 1) * n] for c in range(ncb) for h in range(hb)])

    def unit_params(ref):
        return jnp.stack([ref[:, h * n:(h + 1) * n] for c in range(ncb) for h in range(hb)])

    k = units(k_ref)
    ai = units(ai_ref)
    kk = k * unit_params(kk_ref)
    kk = kk * lax.rsqrt(jnp.maximum(jnp.sum(kk * kk, axis=-1, keepdims=True), 1e-24))
    k = k * (1.0 + (ai - 1.0) * unit_params(ka_ref))
    a = -kk
    b = kk * ai
    lw = units(lw_ref)
    hi, mid, lo = _split3(lw)
    cum = (jnp.einsum("bij,bjk->bik", tri, hi, preferred_element_type=F32)
           + jnp.einsum("bij,bjk->bik", tri, mid, preferred_element_type=F32)
           + jnp.einsum("bij,bjk->bik", tri, lo, preferred_element_type=F32))
    cum_t = cum[:, t - 1:t, :]
    e_in = jnp.exp(cum)
    e_out = jnp.exp(-cum)
    e_prev = jnp.exp(cum - lw)
    e_rem = jnp.exp(cum_t - cum)
    p_t = jnp.exp(cum_t)
    r = units(r_ref)
    v = units(v_ref)
    rh = r * e_in
    ah = a * e_prev
    bh = b * e_out
    kh = k * e_out
    bt = b * e_rem
    kt = k * e_rem
    g = _bmm_nt(jnp.concatenate([ah, rh], axis=1), jnp.concatenate([bh, kh], axis=1))
    row2 = lax.broadcasted_iota(jnp.int32, (t, 2 * t), 0)
    col2 = lax.broadcasted_iota(jnp.int32, (t, 2 * t), 1) % t
    n_cat = jnp.where(col2 < row2, g[:, :t, :], 0.0)
    l_cat = jnp.where(col2 <= row2, g[:, t:, :], 0.0)
    n_ab = n_cat[:, :, :t]
    x = eye + n_ab
    pw = n_ab
    for _ in range((t - 1).bit_length() - 1):
        pw = _bmm(pw, pw)
        x = x + _bmm(x, pw)
    zeros = jnp.zeros_like(v)
    akv = _bmm(n_cat, jnp.concatenate([zeros, v], axis=1))
    wu = _bmm(x, jnp.concatenate([ah, akv], axis=2))
    wu_v = jnp.concatenate([wu, jnp.concatenate([zeros, v], axis=2)], axis=1)
    l2 = _bmm(l_cat, wu_v)
    b2 = _bmm_tn(jnp.concatenate([bt, kt], axis=1), wu_v)
    qt = rh + l2[:, :, :n]
    o0 = l2[:, :, n:]
    mt = jnp.where(drow == dcol, p_t, 0.0) + b2[:, :, :n]
    nt = b2[:, :, n:]
    for c in range(ncb):
        rows = slice(c * t, (c + 1) * t)
        q_ref[rows, :] = jnp.concatenate([qt[c * hb + h] for h in range(hb)], axis=-1)
        o0_ref[rows, :] = jnp.concatenate([o0[c * hb + h] for h in range(hb)], axis=-1)
        mt_ref[:, c] = mt[c * hb:(c + 1) * hb]
        nt_ref[:, c] = nt[c * hb:(c + 1) * hb]


def _bdot3(a, b):
    a_hi = a.astype(BF16)
    a_lo = (a - a_hi.astype(F32)).astype(BF16)
    b_hi = b.astype(BF16)
    b_lo = (b - b_hi.astype(F32)).astype(BF16)
    mm = lambda x, y: jnp.einsum("bij,bjk->bik", x, y, preferred_element_type=F32)
    return mm(a_hi, b_hi) + mm(a_hi, b_lo) + mm(a_lo, b_hi)


def _rwkv_b_kernel(mt_ref, nt_ref, q_ref, o0_ref, r_ref, k_ref, v_ref, ai_ref, g_ref, ka_ref, rk_ref, lw_ref, lb_ref,
                   o_ref, z_ref, *, nh):
    n = RW_HEAD
    c = pl.program_id(0)

    @pl.when(c == 0)
    def _():
        z_ref[...] = jnp.zeros_like(z_ref)

    def heads(ref):
        return jnp.stack([ref[:, h * n:(h + 1) * n] for h in range(nh)])

    z = z_ref[...]
    o = heads(o0_ref) + _bdot3(heads(q_ref), z)
    z_ref[...] = nt_ref[:, 0] + _bdot3(mt_ref[:, 0], z)
    mean = jnp.mean(o, axis=-1, keepdims=True)
    var = jnp.mean(jnp.square(o - mean), axis=-1, keepdims=True)
    o = (o - mean) * lax.rsqrt(var + GN_EPS) * heads(lw_ref) + heads(lb_ref)
    k = heads(k_ref) * (1.0 + (heads(ai_ref) - 1.0) * heads(ka_ref))
    bonus = jnp.sum(heads(r_ref) * k * heads(rk_ref), axis=-1, keepdims=True) * heads(v_ref)
    out = (o + bonus) * heads(g_ref)
    o_ref[...] = jnp.concatenate([out[h] for h in range(nh)], axis=-1).astype(o_ref.dtype)


def rwkv_mix(r, k, v, ai, lw, g, k_k, k_a, r_k, lnx_w, lnx_b):
    s, dim = r.shape
    n = RW_HEAD
    nh = dim // n
    t = RW_CHUNK
    nc = s // t
    hb = _pick(nh, (RW_HEADS_PER_STEP, 4, 2))
    ncb = _pick(nc, (RW_CHUNKS_PER_STEP, 2, 1))
    row = lambda p: p.reshape(1, dim).astype(F32)
    seq_spec = pl.BlockSpec((ncb * t, hb * n), lambda g_, c: (c, g_))
    par_spec = pl.BlockSpec((1, hb * n), lambda g_, c: (0, g_))
    mat_spec = pl.BlockSpec((hb, ncb, n, n), lambda g_, c: (g_, c, 0, 0))
    mt, nt, qt, o0 = pl.pallas_call(
        functools.partial(_rwkv_a_kernel, hb=hb, ncb=ncb),
        out_shape=(jax.ShapeDtypeStruct((nh, nc, n, n), F32), jax.ShapeDtypeStruct((nh, nc, n, n), F32),
                   jax.ShapeDtypeStruct((s, dim), F32), jax.ShapeDtypeStruct((s, dim), F32)),
        grid=(nh // hb, nc // ncb),
        in_specs=[seq_spec] * 5 + [par_spec] * 2,
        out_specs=(mat_spec, mat_spec, seq_spec, seq_spec),
        compiler_params=_params(("parallel", "parallel"), 32 << 20),
        name="rwkv_chunk_local",
    )(r, k, v, ai, lw, row(k_k), row(k_a))
    seq_all = pl.BlockSpec((t, dim), lambda c: (c, 0))
    par_all = pl.BlockSpec((1, dim), lambda c: (0, 0))
    mat_all = pl.BlockSpec((nh, 1, n, n), lambda c: (0, c, 0, 0))
    return pl.pallas_call(
        functools.partial(_rwkv_b_kernel, nh=nh),
        out_shape=jax.ShapeDtypeStruct((s, dim), BF16),
        grid=(nc,),
        in_specs=[mat_all, mat_all] + [seq_all] * 7 + [par_all] * 4,
        out_specs=seq_all,
        scratch_shapes=[pltpu.VMEM((nh, n, n), F32)],
        compiler_params=_params(("arbitrary",), 32 << 20),
        name="rwkv_state_scan",
    )(mt, nt, qt, o0, r, k, v, ai, g, row(k_a), row(r_k), row(lnx_w), row(lnx_b))


def _flash_kernel(qn_ref, qp_ref, kn_ref, kp_ref, v_ref, o_ref, q_sc, m_sc, acc_sc, s0_sc, s1_sc, *, t):
    i = pl.program_id(1)
    nt = t // LANES
    q_sc[...] = jnp.concatenate([qn_ref[...], qp_ref[...]], axis=1)
    m_sc[...] = jnp.full_like(m_sc, -1e30)
    acc_sc[...] = jnp.zeros_like(acc_sc)
    ones = jnp.ones((t, LANES), BF16)

    def scores(j, s_ref):
        start = pl.multiple_of(j * t, t)
        kb = jnp.concatenate([kn_ref[pl.ds(start, t), :], kp_ref[pl.ds(start, t), :]], axis=1)
        s_ref[...] = lax.dot_general(q_sc[...], kb, (((1,), (1,)), ((), ())), preferred_element_type=F32)

    def softmax_pv(j, s_ref, masked):
        start = pl.multiple_of(j * t, t)
        vb = jnp.concatenate([v_ref[pl.ds(start, t), :], ones], axis=1)
        tiles = [s_ref[:, c * LANES:(c + 1) * LANES] for c in range(nt)]
        if masked:
            rows = lax.broadcasted_iota(jnp.int32, (t, LANES), 0)
            cols = lax.broadcasted_iota(jnp.int32, (t, LANES), 1)
            tiles = [jnp.where(cols + c * LANES <= rows, tl, -1e30) for c, tl in enumerate(tiles)]
        mx = tiles[0]
        for c in range(1, nt):
            mx = jnp.maximum(mx, tiles[c])
        m_prev = m_sc[...]
        m_new = jnp.maximum(m_prev, jnp.max(mx, axis=1, keepdims=True))
        alpha = jnp.exp2(m_prev - m_new)
        p = jnp.concatenate([jnp.exp2(tl - m_new).astype(BF16) for tl in tiles], axis=1)
        pv = jnp.dot(p, vb, preferred_element_type=F32)
        acc_sc[...] = jnp.concatenate([alpha, alpha], axis=1) * acc_sc[...] + pv
        m_sc[...] = m_new

    def stage(j, s_cur, s_next):
        scores(j + 1, s_next)
        softmax_pv(j, s_cur, False)

    scores(0, s0_sc)

    def pair(jp, carry):
        stage(2 * jp, s0_sc, s1_sc)
        stage(2 * jp + 1, s1_sc, s0_sc)
        return carry

    lax.fori_loop(0, i // 2, pair, 0)

    @pl.when(i % 2 == 1)
    def _():
        stage(i - 1, s0_sc, s1_sc)
        softmax_pv(i, s1_sc, True)

    @pl.when(i % 2 == 0)
    def _():
        softmax_pv(i, s0_sc, True)

    o_ref[...] = (acc_sc[:, :LANES] / acc_sc[:, LANES:]).astype(o_ref.dtype)


def flash_attention(q_nope, q_pe, kv, k_pe, nh):
    s = q_nope.shape[0]
    dh = LANES
    tq = _pick(s, (FLASH_T, 512, 256, 128))
    vm = 2 * (3 * _nbytes((s, dh), BF16) + 3 * _nbytes((tq, dh), BF16))
    vm += 5 * _nbytes((tq, tq), F32) + 4 * _nbytes((tq, 2 * dh), F32)
    scratch = [pltpu.VMEM((tq, 2 * dh), BF16), pltpu.VMEM((tq, dh), F32), pltpu.VMEM((tq, 2 * dh), F32),
               pltpu.VMEM((tq, tq), F32), pltpu.VMEM((tq, tq), F32)]
    return pl.pallas_call(
        functools.partial(_flash_kernel, t=tq),
        out_shape=jax.ShapeDtypeStruct((s, nh * dh), BF16),
        grid=(nh, s // tq),
        in_specs=[pl.BlockSpec((tq, dh), lambda h, i: (i, h)),
                  pl.BlockSpec((tq, dh), lambda h, i: (i, h)),
                  pl.BlockSpec((s, dh), lambda h, i: (0, h)),
                  pl.BlockSpec((s, dh), lambda h, i: (0, 0)),
                  pl.BlockSpec((s, dh), lambda h, i: (0, nh + h))],
        out_specs=pl.BlockSpec((tq, dh), lambda h, i: (i, h)),
        scratch_shapes=scratch,
        compiler_params=_params(("parallel", "arbitrary"), vm),
        name="mla_flash_attention",
    )(q_nope, q_pe, kv, k_pe, kv)


def _spatial_kernel(w_ref, bias_ref, v_ref, u_ref, o_ref, *, nch):
    c = CHUNK
    row = lax.broadcasted_iota(jnp.int32, (c, c), 0)
    col = lax.broadcasted_iota(jnp.int32, (c, c), 1)
    w = jnp.where(col <= row, w_ref[0], 0.0).astype(BF16)
    bias = bias_ref[0]
    for ci in range(nch):
        sl = pl.ds(ci * c, c)
        sg = jnp.dot(w, v_ref[sl, :], preferred_element_type=F32) + bias
        o_ref[sl, :] = (u_ref[sl, :].astype(F32) * sg).astype(o_ref.dtype)


def spatial_gate(z, vn, w_sp, b_sp):
    s, d = vn.shape
    g = w_sp.shape[0]
    gd = d // g
    nch = _pick(s // CHUNK, (16, 8, 4, 2, 1))
    rows = nch * CHUNK
    bias = jnp.broadcast_to(b_sp.astype(F32)[:, :, None], (g, CHUNK, gd))
    return pl.pallas_call(
        functools.partial(_spatial_kernel, nch=nch),
        out_shape=jax.ShapeDtypeStruct((s, d), BF16),
        grid=(g, s // rows),
        in_specs=[pl.BlockSpec((1, CHUNK, CHUNK), lambda gi, ci: (gi, 0, 0)),
                  pl.BlockSpec((1, CHUNK, gd), lambda gi, ci: (gi, 0, 0)),
                  pl.BlockSpec((rows, gd), lambda gi, ci: (ci, gi)),
                  pl.BlockSpec((rows, gd), lambda gi, ci: (ci, gi))],
        out_specs=pl.BlockSpec((rows, gd), lambda gi, ci: (ci, gi)),
        compiler_params=_params(("parallel", "parallel"), 16 << 20),
        name="spatial_gate",
    )(w_sp.astype(F32), bias, vn, z)


def _norm_router_kernel(h_ref, g_ref, wr_ref, xn_ref, lg_ref, *, eps):
    x = h_ref[...]
    ms = jnp.mean(x * x, axis=-1, keepdims=True)
    xn = x * lax.rsqrt(ms + eps) * g_ref[...]
    xn_ref[...] = xn
    x_hi = xn.astype(BF16)
    x_lo = (xn - x_hi.astype(F32)).astype(BF16)
    w = wr_ref[...]
    w_hi = w.astype(BF16)
    w_lo = (w - w_hi.astype(F32)).astype(BF16)
    lg_ref[...] = (jnp.dot(x_hi, w_hi, preferred_element_type=F32) + jnp.dot(x_hi, w_lo, preferred_element_type=F32)
                   + jnp.dot(x_lo, w_hi, preferred_element_type=F32))


def norm_router(h, g, w_router):
    m, d = h.shape
    e = w_router.shape[1]
    wr = jnp.pad(w_router.astype(F32), ((0, 0), (0, LANES - e)))
    tm = _pick(m, (512, 256, 128, 64, 32, 16, 8))
    vm = 2 * (2 * _nbytes((tm, d), F32) + _nbytes((d, LANES), F32)) + 6 * _nbytes((tm, d), F32)
    xn, lg = pl.pallas_call(
        functools.partial(_norm_router_kernel, eps=RMS_EPS),
        out_shape=(jax.ShapeDtypeStruct((m, d), F32), jax.ShapeDtypeStruct((m, LANES), F32)),
        grid=(m // tm,),
        in_specs=[pl.BlockSpec((tm, d), lambda i: (i, 0)),
                  pl.BlockSpec((1, d), lambda i: (0, 0)),
                  pl.BlockSpec((d, LANES), lambda i: (0, 0))],
        out_specs=(pl.BlockSpec((tm, d), lambda i: (i, 0)), pl.BlockSpec((tm, LANES), lambda i: (i, 0))),
        compiler_params=_params(("parallel",), vm),
        name="norm_router",
    )(h, g.reshape(1, d).astype(F32), wr)
    return xn, lg[:, :e]


def _row_copy(src_hbm, src_row, dst, dst_row, sem):
    return pltpu.make_async_copy(src_hbm.at[pl.ds(src_row, 1)], dst.at[pl.ds(dst_row, 1)], sem)


def _moe_up_kernel(te_ref, tv_ref, src_ref, x_hbm, wg_ref, wu_ref, gate_ref, o_ref, xbuf, xb16, sems, *, tile):
    t = pl.program_id(0)
    n_tiles = pl.num_programs(0)
    slot = t % 2

    def start_tile(tt, sl):
        def body(r, carry):
            _row_copy(x_hbm, src_ref[tt * tile + r], xbuf.at[sl], r, sems.at[sl]).start()
            return carry
        lax.fori_loop(0, tile, body, 0)

    def wait_tile(sl):
        def body(r, carry):
            _row_copy(x_hbm, 0, xbuf.at[sl], r, sems.at[sl]).wait()
            return carry
        lax.fori_loop(0, tile, body, 0)

    @pl.when(pl.program_id(1) == 0)
    def _():
        @pl.when(t == 0)
        def _():
            start_tile(0, 0)

        wait_tile(slot)

        @pl.when(t + 1 < n_tiles)
        def _():
            start_tile(t + 1, 1 - slot)

        xb16[...] = xbuf[slot].astype(BF16)

    @pl.when(tv_ref[t] == 1)
    def _():
        x = xb16[...]
        g = jnp.dot(x, wg_ref[...], preferred_element_type=F32)
        u = jnp.dot(x, wu_ref[...], preferred_element_type=F32)
        o_ref[...] = (gate_ref[...] * (g * jax.nn.sigmoid(g) * u)).astype(o_ref.dtype)

    @pl.when(tv_ref[t] == 0)
    def _():
        o_ref[...] = jnp.zeros_like(o_ref)


def _moe_down_kernel(te_ref, tv_ref, a_ref, w_ref, o_ref):
    t = pl.program_id(0)

    @pl.when(tv_ref[t] == 1)
    def _():
        o_ref[...] = jnp.dot(a_ref[...], w_ref[...], preferred_element_type=F32)

    @pl.when(tv_ref[t] == 0)
    def _():
        o_ref[...] = jnp.zeros_like(o_ref)


def _moe_combine_kernel(s1_ref, s2_ref, h_ref, g_ref, d_hbm, o_ref, on_ref, buf, sem, *, tm):
    base = pl.program_id(0) * tm

    def start(r, carry):
        _row_copy(d_hbm, s1_ref[base + r], buf.at[0], r, sem).start()
        _row_copy(d_hbm, s2_ref[base + r], buf.at[1], r, sem).start()
        return carry

    lax.fori_loop(0, tm, start, 0)

    def wait(r, carry):
        _row_copy(d_hbm, 0, buf.at[0], r, sem).wait()
        _row_copy(d_hbm, 0, buf.at[1], r, sem).wait()
        return carry

    lax.fori_loop(0, tm, wait, 0)
    o = h_ref[...] + buf[0] + buf[1]
    o_ref[...] = o
    ms = jnp.mean(o * o, axis=-1, keepdims=True)
    on_ref[...] = (o * lax.rsqrt(ms + RMS_EPS) * g_ref[...]).astype(on_ref.dtype)


def _route(logits, tile):
    s, ne = logits.shape
    top_val, top_idx = lax.top_k(logits, TOP_K)
    top_w = jax.nn.softmax(top_val, axis=-1)
    e_flat = top_idx.reshape(-1).astype(jnp.int32)
    onehot = (e_flat[:, None] == jnp.arange(ne, dtype=jnp.int32)[None, :]).astype(jnp.int32)
    csum = jnp.cumsum(onehot, axis=0)
    rank = jnp.sum(csum * onehot, axis=1) - 1
    counts = csum[-1]
    padded = (counts + tile - 1) // tile * tile
    ends = jnp.cumsum(padded)
    slot = (ends - padded)[e_flat] + rank
    p_rows = s * TOP_K + ne * tile
    tok = (jnp.arange(s * TOP_K, dtype=jnp.int32) // TOP_K).astype(F32)
    placed = jnp.zeros((p_rows, 2), F32).at[slot].set(jnp.stack([tok, top_w.reshape(-1)], axis=1))
    src = placed[:, 0].astype(jnp.int32)
    gate = placed[:, 1]
    tstart = jnp.arange(p_rows // tile, dtype=jnp.int32) * tile
    tile_e = jnp.minimum(jnp.sum((tstart[:, None] >= ends[None, :]).astype(jnp.int32), axis=1), ne - 1)
    tile_valid = (tstart < ends[-1]).astype(jnp.int32)
    return src, gate.reshape(p_rows, 1), tile_e, tile_valid, slot.reshape(s, TOP_K)


def _rw_prep_kernel(yr_ref, yk_ref, yv_ref, yl_ref, pr_ref, pk_ref, pv_ref, pl_ref,
                    mur_ref, muk_ref, muv_ref, mul_ref, w2_ref, a2_ref, g2_ref, w0_ref, a0_ref,
                    r_ref, k_ref, v_ref, lw_ref, ai_ref, g_ref):
    first = pl.program_id(0) == 0

    def shifted(x_ref, p_ref, mu_ref):
        x = x_ref[...].astype(F32)
        prev = jnp.where(first, 0.0, p_ref[...].astype(F32)[7:8, :])
        row = lax.broadcasted_iota(jnp.int32, x.shape, 0)
        xp = jnp.where(row == 0, prev, pltpu.roll(x, 1, axis=0))
        return x + (xp - x) * mu_ref[...]

    r_ref[...] = shifted(yr_ref, pr_ref, mur_ref)
    k_ref[...] = shifted(yk_ref, pk_ref, muk_ref)
    v_ref[...] = shifted(yv_ref, pv_ref, muv_ref)
    lo = shifted(yl_ref, pl_ref, mul_ref)
    lane = lax.broadcasted_iota(jnp.int32, lo.shape, 1)
    act = jnp.where(lane < DECAY_LORA, jnp.tanh(lo),
                    jnp.where(lane < DECAY_LORA + AAA_LORA, lo,
                              jnp.where(lane < DECAY_LORA + AAA_LORA + GATE_LORA, jax.nn.sigmoid(lo), 0.0)))
    act = act.astype(BF16)
    w_pre = w0_ref[...] + jnp.dot(act, w2_ref[...], preferred_element_type=F32)
    x = -w_pre
    softplus = jnp.maximum(x, 0.0) + jnp.log(1.0 + jnp.exp(-jnp.abs(x)))
    lw_ref[...] = -jnp.exp(-softplus - 0.5)
    ai_ref[...] = jax.nn.sigmoid(a0_ref[...] + jnp.dot(act, a2_ref[...], preferred_element_type=F32))
    g_ref[...] = jnp.dot(act, g2_ref[...], preferred_element_type=F32)


def _rwkv_group(y, col_lo, mu, w0, w2, a0, a2, g2, k_k, k_a, r_k, lnx_w, lnx_b):
    s = y.shape[0]
    dim = RW_DIM
    lora = DECAY_LORA + AAA_LORA + GATE_LORA
    lwid = -(-lora // LANES) * LANES
    assert col_lo % lwid == 0 and dim % LANES == 0
    tm = _pick(s, (128, 64, 32, 16, 8))
    pb = tm // 8
    mu_rkv = mu[:3 * dim].reshape(1, 3 * dim)
    mu_lo = jnp.pad(mu[3 * dim:], (0, lwid - lora)).reshape(1, lwid)

    def pad_rows(w, start):
        return jnp.pad(w, ((start, lwid - start - w.shape[0]), (0, 0))).astype(BF16)

    y_spec = lambda c: pl.BlockSpec((tm, dim), lambda i: (i, c))
    p_spec = lambda c: pl.BlockSpec((8, dim), lambda i: (jnp.maximum(i * pb - 1, 0), c))
    mu_spec = lambda c: pl.BlockSpec((1, dim), lambda i: (0, c))
    cl = col_lo // lwid
    w_spec = pl.BlockSpec((lwid, dim), lambda i: (0, 0))
    row_spec = pl.BlockSpec((1, dim), lambda i: (0, 0))
    out_spec = pl.BlockSpec((tm, dim), lambda i: (i, 0))
    out_sds = jax.ShapeDtypeStruct((s, dim), F32)
    vm = 2 * (3 * _nbytes((tm, dim), BF16) + 6 * _nbytes((tm, dim), F32) + 3 * _nbytes((lwid, dim), BF16))
    vm += 6 * _nbytes((tm, dim), F32)
    r, k, v, lw, ai, g = pl.pallas_call(
        _rw_prep_kernel,
        out_shape=(out_sds,) * 6,
        grid=(s // tm,),
        in_specs=[y_spec(0), y_spec(1), y_spec(2), pl.BlockSpec((tm, lwid), lambda i: (i, cl)),
                  p_spec(0), p_spec(1), p_spec(2),
                  pl.BlockSpec((8, lwid), lambda i: (jnp.maximum(i * pb - 1, 0), cl)),
                  mu_spec(0), mu_spec(1), mu_spec(2), pl.BlockSpec((1, lwid), lambda i: (0, 0)),
                  w_spec, w_spec, w_spec, row_spec, row_spec],
        out_specs=(out_spec,) * 6,
        compiler_params=_params(("parallel",), vm),
        name="rwkv_prep",
    )(y, y, y, y, y, y, y, y, mu_rkv, mu_rkv, mu_rkv, mu_lo,
      pad_rows(w2, 0), pad_rows(a2, DECAY_LORA), pad_rows(g2, DECAY_LORA + AAA_LORA),
      w0.reshape(1, dim), a0.reshape(1, dim))
    return rwkv_mix(r, k, v, ai, lw, g, k_k, k_a, r_k.reshape(-1), lnx_w, lnx_b)


def _rope(x, cos, sin):
    half = x.shape[-1] // 2
    x1, x2 = x[..., :half], x[..., half:]
    return jnp.concatenate([x1 * cos - x2 * sin, x2 * cos + x1 * sin], axis=-1)


def _mla_group(y, col_q, col_kv, col_pe, cos, sin, q_norm, w_uq, kv_norm, w_ukv):
    s = y.shape[0]
    nh = MLA_HEADS
    cqn = rmsnorm(y, q_norm, BF16, width=Q_LORA, col_block=col_q // Q_LORA)
    ckvn = rmsnorm(y, kv_norm, BF16, width=KV_LORA, col_block=col_kv // KV_LORA)
    assert QK_NOPE == LANES and V_HEAD == LANES and QK_ROPE <= LANES
    half = QK_ROPE // 2
    scale = (QK_NOPE + QK_ROPE) ** -0.5 * 1.4426950408889634
    wq = w_uq.reshape(Q_LORA, nh, QK_NOPE + QK_ROPE)
    w_nope = wq[:, :, :QK_NOPE].reshape(Q_LORA, nh * LANES).astype(BF16)
    w_pe = wq[:, :, QK_NOPE:]
    w_rot = jnp.concatenate([-w_pe[..., half:], w_pe[..., :half]], axis=-1)
    lane_pad = ((0, 0), (0, 0), (0, LANES - QK_ROPE))
    w_pe = jnp.pad(w_pe, lane_pad).reshape(Q_LORA, nh * LANES).astype(BF16)
    w_rot = jnp.pad(w_rot, lane_pad).reshape(Q_LORA, nh * LANES).astype(BF16)
    tm = _pick(s, (1024, 512, 256, 128, 64, 32, 16, 8))
    tn = _pick(nh * LANES, (512, 256, 128))
    table_pad = ((0, 0), (0, LANES - QK_ROPE))
    cos_t = jnp.tile(jnp.pad(jnp.concatenate([cos, cos], axis=-1), table_pad), (1, tn // LANES))
    sin_t = jnp.tile(jnp.pad(jnp.concatenate([sin, sin], axis=-1), table_pad), (1, tn // LANES))
    q_nope = matmul(cqn, [w_nope], lambda accs, extras: accs[0] * scale, BF16, tm=tm, tn=tn, name="mla_q_nope_up")
    q_pe = matmul(cqn, [w_pe, w_rot], lambda accs, extras: (accs[0] * extras[0] + accs[1] * extras[1]) * scale,
                  BF16, tm=tm, tn=tn,
                  extras=[(cos_t, (tm, tn), lambda i, jn: (i, 0)), (sin_t, (tm, tn), lambda i, jn: (i, 0))],
                  name="mla_q_rope_up")
    w_kv = jnp.transpose(w_ukv.reshape(KV_LORA, nh, 2, LANES), (0, 2, 1, 3)).reshape(KV_LORA, 2 * nh * LANES)
    kv = matmul(ckvn, [w_kv.astype(BF16)], _epi_id, BF16, tm=tm, tn=tn, name="mla_kv_up")
    k_pe = _rope(y[:, col_pe:col_pe + QK_ROPE].astype(F32), cos, sin)
    k_pe = jnp.pad(k_pe, table_pad).astype(BF16)
    return flash_attention(q_nope, q_pe, kv, k_pe, nh)


def _even_layer(h, hn, j, cos, sin, prm):
    w_in = prm["ev_in_proj"][j]
    rw_end = 3 * RW_DIM
    lora_w = DECAY_LORA + AAA_LORA + GATE_LORA
    rw_proj = rw_end + lora_w
    w_perm = jnp.concatenate([
        w_in[:, :rw_end],
        w_in[:, rw_proj:rw_proj + Q_LORA + KV_LORA],
        w_in[:, rw_end:rw_proj],
        w_in[:, rw_proj + Q_LORA + KV_LORA:],
    ], axis=1).astype(BF16)
    y = matmul(hn, [w_perm], _epi_id, BF16, tm=1024, tn=512, name="even_in_proj")
    col_q = rw_end
    col_kv = rw_end + Q_LORA
    col_lo = rw_end + Q_LORA + KV_LORA
    col_pe = col_lo + lora_w
    mu = prm["rw_mu"][j]
    o_rw = _rwkv_group(y, col_lo, mu, prm["rw_w0"][j], prm["rw_w2"][j], prm["rw_a0"][j], prm["rw_a2"][j],
                       prm["rw_g2"][j], prm["rw_k_k"][j], prm["rw_k_a"][j], prm["rw_r_k"][j],
                       prm["rw_lnx_w"][j], prm["rw_lnx_b"][j])
    o_mla = _mla_group(y, col_q, col_kv, col_pe, cos, sin, prm["mla_q_norm"][j], prm["mla_w_uq"][j],
                       prm["mla_kv_norm"][j], prm["mla_w_ukv"][j])
    return mix_out_proj(o_rw, o_mla, prm["ev_out_proj"][j].astype(BF16), h)


def _dense_ffn(h, hn, wg, wu, wd):
    d, f = wg.shape
    fp = -(-f // 1024) * 1024 if f > 1024 else f
    pad = fp - f
    wg = jnp.pad(wg.astype(BF16), ((0, 0), (0, pad)))
    wu = jnp.pad(wu.astype(BF16), ((0, 0), (0, pad)))
    wd = jnp.pad(wd.astype(BF16), ((0, pad), (0, 0)))
    mid = matmul(hn, [wg, wu], _epi_swiglu, BF16, tm=1024, tn=512, name="ffn_up")
    tm = _pick(h.shape[0], (1024, 512, 256, 128, 64, 32, 16, 8))
    tn = _pick(h.shape[1], (1024, 512, 256, 128))
    tk = _pick(fp, (2816, 2048, 1024, 512, 256, 128))
    nk = fp // tk
    return matmul(mid, [wd], _epi_residual, F32, tm=tm, tn=tn, tk=tk,
                  extras=[_res_extra(h, tm, tn, nk)], name="ffn_down")


def _moe(h, g_norm, w_router, wg, wu, wd, g_next):
    ne, d, f = wg.shape
    s = h.shape[0]
    xn, logits = norm_router(h, g_norm, w_router)
    tile = _pick(s, (MOE_TILE, 256, 128, 64, 32, 16, 8))
    src, gate, tile_e, tile_valid, slots = _route(logits, tile)
    p_rows = src.shape[0]
    n_tiles = p_rows // tile
    tn = _pick(f, (512, 256, 128))
    vm = 2 * (2 * _nbytes((d, tn), BF16) + _nbytes((tile, tn), BF16))
    vm += 2 * _nbytes((tile, d), F32) + 2 * _nbytes((tile, d), BF16) + 4 * _nbytes((tile, tn), F32)
    w_spec = pl.BlockSpec((None, d, tn), lambda t, jn, te, tv, sr: (te[t], 0, jn))
    mid = pl.pallas_call(
        functools.partial(_moe_up_kernel, tile=tile),
        out_shape=jax.ShapeDtypeStruct((p_rows, f), BF16),
        grid_spec=pltpu.PrefetchScalarGridSpec(
            num_scalar_prefetch=3, grid=(n_tiles, f // tn),
            in_specs=[pl.BlockSpec(memory_space=pl.ANY), w_spec, w_spec,
                      pl.BlockSpec((tile, 1), lambda t, jn, te, tv, sr: (t, 0))],
            out_specs=pl.BlockSpec((tile, tn), lambda t, jn, te, tv, sr: (t, jn)),
            scratch_shapes=[pltpu.VMEM((2, tile, d), F32), pltpu.VMEM((tile, d), BF16),
                            pltpu.SemaphoreType.DMA((2,))]),
        compiler_params=_params(("arbitrary", "arbitrary"), vm),
        name="moe_up",
    )(tile_e, tile_valid, src, xn, wg.astype(BF16), wu.astype(BF16), gate)
    tn2 = _pick(d, (1024, 512, 256, 128))
    vm = 2 * (_nbytes((tile, f), BF16) + _nbytes((f, tn2), BF16) + _nbytes((tile, tn2), F32)) + _nbytes((tile, tn2), F32)
    dn = pl.pallas_call(
        _moe_down_kernel,
        out_shape=jax.ShapeDtypeStruct((p_rows, d), F32),
        grid_spec=pltpu.PrefetchScalarGridSpec(
            num_scalar_prefetch=2, grid=(n_tiles, d // tn2),
            in_specs=[pl.BlockSpec((tile, f), lambda t, jn, te, tv: (t, 0)),
                      pl.BlockSpec((None, f, tn2), lambda t, jn, te, tv: (te[t], 0, jn))],
            out_specs=pl.BlockSpec((tile, tn2), lambda t, jn, te, tv: (t, jn))),
        compiler_params=_params(("parallel", "parallel"), vm),
        name="moe_down",
    )(tile_e, tile_valid, mid, wd.astype(BF16))
    tm = _pick(s, (256, 128, 64, 32, 16, 8))
    vm = 4 * _nbytes((tm, d), F32) + 2 * _nbytes((tm, d), F32) + 2 * _nbytes((tm, d), F32)
    return pl.pallas_call(
        functools.partial(_moe_combine_kernel, tm=tm),
        out_shape=(jax.ShapeDtypeStruct((s, d), F32), jax.ShapeDtypeStruct((s, d), BF16)),
        grid_spec=pltpu.PrefetchScalarGridSpec(
            num_scalar_prefetch=2, grid=(s // tm,),
            in_specs=[pl.BlockSpec((tm, d), lambda i, s1, s2: (i, 0)), pl.BlockSpec((1, d), lambda i, s1, s2: (0, 0)),
                      pl.BlockSpec(memory_space=pl.ANY)],
            out_specs=(pl.BlockSpec((tm, d), lambda i, s1, s2: (i, 0)), pl.BlockSpec((tm, d), lambda i, s1, s2: (i, 0))),
            scratch_shapes=[pltpu.VMEM((2, tm, d), F32), pltpu.SemaphoreType.DMA(())]),
        compiler_params=_params(("arbitrary",), vm),
        name="moe_combine",
    )(slots[:, 0], slots[:, 1], h, g_next.reshape(1, d).astype(F32), dn)


def _odd_layer(h, hn, j, prm):
    z = matmul(hn, [prm["sg_in_proj"][j].astype(BF16)], _epi_gelu, BF16, tm=1024, tn=512, name="sg_in_proj")
    d = h.shape[1]
    vn = layernorm(z, prm["sg_ln_w"][j], prm["sg_ln_b"][j], BF16, width=d, col_block=1)
    gated = spatial_gate(z, vn, prm["sg_w_spatial"][j], prm["sg_b_spatial"][j])
    tm = _pick(h.shape[0], (1024, 512, 256, 128, 64, 32, 16, 8))
    tn = _pick(d, (512, 256, 128))
    return matmul(gated, [prm["sg_out_proj"][j].astype(BF16)], _epi_residual, F32, tm=tm, tn=tn,
                  extras=[_res_extra(h, tm, tn, 1)], name="sg_out_proj")


def kernel(x, p, positions, norm_mix, norm_ffn, norm_ple, final_norm, ev_in_proj, rw_mu, rw_w0, rw_w2, rw_a0, rw_a2, rw_g2, rw_k_k, rw_k_a, rw_r_k, rw_lnx_w, rw_lnx_b, mla_q_norm, mla_w_uq, mla_kv_norm, mla_w_ukv, ev_out_proj, ffn_w_gate, ffn_w_up, ffn_w_down, sg_in_proj, sg_ln_w, sg_ln_b, sg_w_spatial, sg_b_spatial, sg_out_proj, moe_router, moe_w_gate, moe_w_up, moe_w_down, ple_proj, ple_gate):
    prm = dict(ev_in_proj=ev_in_proj, rw_mu=rw_mu, rw_w0=rw_w0, rw_w2=rw_w2, rw_a0=rw_a0, rw_a2=rw_a2, rw_g2=rw_g2,
               rw_k_k=rw_k_k, rw_k_a=rw_k_a, rw_r_k=rw_r_k, rw_lnx_w=rw_lnx_w, rw_lnx_b=rw_lnx_b,
               mla_q_norm=mla_q_norm, mla_w_uq=mla_w_uq, mla_kv_norm=mla_kv_norm, mla_w_ukv=mla_w_ukv,
               ev_out_proj=ev_out_proj, sg_in_proj=sg_in_proj, sg_ln_w=sg_ln_w, sg_ln_b=sg_ln_b,
               sg_w_spatial=sg_w_spatial, sg_b_spatial=sg_b_spatial, sg_out_proj=sg_out_proj)
    bsz, s, d = x.shape
    assert bsz == 1
    inv = 1.0 / (ROPE_THETA ** (jnp.arange(0, QK_ROPE, 2, dtype=F32) / QK_ROPE))
    ang = positions[0].astype(F32)[:, None] * inv
    cos, sin = jnp.cos(ang), jnp.sin(ang)
    h = x[0]
    for i in range(DEPTH):
        j = i // 2
        if i % 2 == 0:
            hn = rmsnorm(h, norm_mix[i], BF16)
            h = _even_layer(h, hn, j, cos, sin, prm)
            hn = rmsnorm(h, norm_ffn[i], BF16)
            h = _dense_ffn(h, hn, ffn_w_gate[j], ffn_w_up[j], ffn_w_down[j])
        else:
            hn = rmsnorm(h, norm_mix[i], BF16)
            h = _odd_layer(h, hn, j, prm)
            h, hn = _moe(h, norm_ffn[i], moe_router[j], moe_w_gate[j], moe_w_up[j], moe_w_down[j], norm_ple[i])
        if i % 2 == 0:
            hn = rmsnorm(h, norm_ple[i], BF16)
        h = ple(hn, ple_gate[i].astype(BF16), p[i, 0].astype(BF16), ple_proj[i].astype(BF16), h)
    return rmsnorm(h, final_norm, F32)[None]
```

```python
import functools

import jax
import jax.numpy as jnp
from jax import lax
from jax.experimental import pallas as pl
from jax.experimental.pallas import tpu as pltpu

D_MODEL = 4096
DEPTH = 4
RW_HEADS = 32
RW_HEAD = 64
RW_DIM = RW_HEADS * RW_HEAD
DECAY_LORA = 96
AAA_LORA = 96
GATE_LORA = 256
GN_EPS = 64e-5
MLA_HEADS = 16
Q_LORA = 1536
KV_LORA = 512
QK_NOPE = 128
QK_ROPE = 64
V_HEAD = 128
ROPE_THETA = 10000.0
CHUNK = 128
SG_GROUPS = 32
LN_EPS = 1e-5
D_FF = 11008
N_EXPERTS = 8
TOP_K = 2
D_FF_EXPERT = 1536
PLE_DIM = 256
RMS_EPS = 1e-6

V7X_VMEM_BYTES = 64 * 1024 * 1024
LANES = 128
RW_CHUNK = 64
RW_HEADS_PER_STEP = 8
RW_CHUNKS_PER_STEP = 4
FLASH_T = 1024
MOE_TILE = 512

F32 = jnp.float32
BF16 = jnp.bfloat16


def _pick(dim, candidates):
    for c in candidates:
        if dim % c == 0:
            return c
    return dim


def _params(sem, vmem_bytes, flags=None):
    limit = int(min(V7X_VMEM_BYTES - 6 * 1024 * 1024, max(vmem_bytes * 5 // 4 + (2 << 20), 16 << 20)))
    return pltpu.CompilerParams(dimension_semantics=sem, vmem_limit_bytes=limit, flags=flags)


def _nbytes(shape, dtype):
    n = 1
    for s in shape:
        n *= s
    return n * jnp.dtype(dtype).itemsize


def _rmsnorm_kernel(x_ref, g_ref, o_ref, *, eps):
    x = x_ref[...].astype(F32)
    ms = jnp.mean(x * x, axis=-1, keepdims=True)
    o_ref[...] = (x * lax.rsqrt(ms + eps) * g_ref[...]).astype(o_ref.dtype)


def rmsnorm(x, g, out_dtype, *, width=None, col_block=0):
    m = x.shape[0]
    d = x.shape[1] if width is None else width
    tm = _pick(m, (512, 256, 128, 64, 32, 16, 8))
    vm = 2 * (_nbytes((tm, d), x.dtype) + _nbytes((tm, d), out_dtype)) + 4 * _nbytes((tm, d), F32)
    return pl.pallas_call(
        functools.partial(_rmsnorm_kernel, eps=RMS_EPS),
        out_shape=jax.ShapeDtypeStruct((m, d), out_dtype),
        grid=(m // tm,),
        in_specs=[pl.BlockSpec((tm, d), lambda i: (i, col_block)),
                  pl.BlockSpec((1, d), lambda i: (0, 0))],
        out_specs=pl.BlockSpec((tm, d), lambda i: (i, 0)),
        compiler_params=_params(("parallel",), vm),
        name="rmsnorm",
    )(x, g.reshape(1, d).astype(F32))


def _layernorm_kernel(x_ref, w_ref, b_ref, o_ref, *, eps):
    x = x_ref[...].astype(F32)
    mean = jnp.mean(x, axis=-1, keepdims=True)
    var = jnp.mean(jnp.square(x - mean), axis=-1, keepdims=True)
    o_ref[...] = ((x - mean) * lax.rsqrt(var + eps) * w_ref[...] + b_ref[...]).astype(o_ref.dtype)


def layernorm(x, w, b, out_dtype, *, width, col_block):
    m = x.shape[0]
    tm = _pick(m, (512, 256, 128, 64, 32, 16, 8))
    vm = 2 * (_nbytes((tm, width), x.dtype) + _nbytes((tm, width), out_dtype)) + 4 * _nbytes((tm, width), F32)
    return pl.pallas_call(
        functools.partial(_layernorm_kernel, eps=LN_EPS),
        out_shape=jax.ShapeDtypeStruct((m, width), out_dtype),
        grid=(m // tm,),
        in_specs=[pl.BlockSpec((tm, width), lambda i: (i, col_block)),
                  pl.BlockSpec((1, width), lambda i: (0, 0)),
                  pl.BlockSpec((1, width), lambda i: (0, 0))],
        out_specs=pl.BlockSpec((tm, width), lambda i: (i, 0)),
        compiler_params=_params(("parallel",), vm),
        name="layernorm",
    )(x, w.reshape(1, width).astype(F32), b.reshape(1, width).astype(F32))


def _mm_kernel(*refs, nk, n_w, n_extra, epi):
    a_ref = refs[0]
    w_refs = refs[1:1 + n_w]
    extra_refs = refs[1 + n_w:1 + n_w + n_extra]
    o_ref = refs[1 + n_w + n_extra]
    acc_refs = refs[2 + n_w + n_extra:]
    a = a_ref[...]
    if nk == 1:
        accs = [jnp.dot(a, w[...], preferred_element_type=F32) for w in w_refs]
        o_ref[...] = epi(accs, [e[...] for e in extra_refs]).astype(o_ref.dtype)
        return
    k = pl.program_id(2)

    @pl.when(k == 0)
    def _():
        for acc in acc_refs:
            acc[...] = jnp.zeros_like(acc)

    for acc, w in zip(acc_refs, w_refs):
        acc[...] += jnp.dot(a, w[...], preferred_element_type=F32)

    @pl.when(k == nk - 1)
    def _():
        o_ref[...] = epi([acc[...] for acc in acc_refs], [e[...] for e in extra_refs]).astype(o_ref.dtype)


def matmul(a, ws, epi, out_dtype, *, extras=(), tm=1024, tn=512, tk=None, a_col_block=0, a_width=None,
           w_specs=None, n_out=None, name="matmul"):
    m = a.shape[0]
    kdim = a.shape[1] if a_width is None else a_width
    n = ws[0].shape[-1] if n_out is None else n_out
    tm = _pick(m, (tm, 512, 256, 128, 64, 32, 16, 8))
    tn = _pick(n, (tn, 512, 256, 128))
    tk = kdim if tk is None else _pick(kdim, (tk,))
    nk = kdim // tk
    kb0 = a_col_block * nk
    if nk == 1:
        grid = (m // tm, n // tn)
        a_spec = pl.BlockSpec((tm, tk), lambda i, j: (i, kb0))
        w_spec_list = w_specs or [pl.BlockSpec((tk, tn), lambda i, j: (0, j)) for _ in ws]
        o_spec = pl.BlockSpec((tm, tn), lambda i, j: (i, j))
        sem = ("parallel", "parallel")
        scratch = []
    else:
        grid = (m // tm, n // tn, nk)
        a_spec = pl.BlockSpec((tm, tk), lambda i, j, k: (i, kb0 + k))
        w_spec_list = w_specs or [pl.BlockSpec((tk, tn), lambda i, j, k: (k, j)) for _ in ws]
        o_spec = pl.BlockSpec((tm, tn), lambda i, j, k: (i, j))
        sem = ("parallel", "parallel", "arbitrary")
        scratch = [pltpu.VMEM((tm, tn), F32) for _ in ws]
    extra_specs = [pl.BlockSpec(bs, im) for (_, bs, im) in extras]
    vm = 2 * _nbytes((tm, tk), a.dtype) + 2 * len(ws) * _nbytes((tk, tn), ws[0].dtype)
    vm += 2 * _nbytes((tm, tn), out_dtype) + (len(ws) + 2) * _nbytes((tm, tn), F32)
    for (arr, bs, _) in extras:
        vm += 2 * _nbytes([1 if b is None else b for b in bs], arr.dtype)
    return pl.pallas_call(
        functools.partial(_mm_kernel, nk=nk, n_w=len(ws), n_extra=len(extras), epi=epi),
        out_shape=jax.ShapeDtypeStruct((m, n), out_dtype),
        grid=grid,
        in_specs=[a_spec] + list(w_spec_list) + extra_specs,
        out_specs=o_spec,
        scratch_shapes=scratch,
        compiler_params=_params(sem, vm),
        name=name,
    )(a, *ws, *[e[0] for e in extras])


def _epi_id(accs, extras):
    return accs[0]


def _epi_residual(accs, extras):
    return extras[0] + accs[0]


def _epi_swiglu(accs, extras):
    g, u = accs
    return g * jax.nn.sigmoid(g) * u


def _epi_gelu(accs, extras):
    z = accs[0]
    return 0.5 * z * (1.0 + lax.erf(z * (2.0 ** -0.5)))


def _res_extra(h, tm, tn, nk):
    if nk == 1:
        return (h, (tm, tn), lambda i, j: (i, j))
    return (h, (tm, tn), lambda i, j, k: (i, j))


def _mix_out_kernel(a1_ref, a2_ref, w1_ref, w2_ref, h_ref, o_ref):
    o_ref[...] = (h_ref[...] + jnp.dot(a1_ref[...], w1_ref[...], preferred_element_type=F32)
                  + jnp.dot(a2_ref[...], w2_ref[...], preferred_element_type=F32))


def mix_out_proj(a1, a2, w, h):
    m, kh = a1.shape
    n = w.shape[1]
    assert a2.shape == a1.shape and w.shape[0] == 2 * kh
    tm = _pick(m, (1024, 512, 256, 128, 64, 32, 16, 8))
    tn = _pick(n, (512, 256, 128))
    vm = 2 * (2 * _nbytes((tm, kh), BF16) + 2 * _nbytes((kh, tn), BF16) + 2 * _nbytes((tm, tn), F32))
    vm += 3 * _nbytes((tm, tn), F32)
    return pl.pallas_call(
        _mix_out_kernel,
        out_shape=jax.ShapeDtypeStruct((m, n), F32),
        grid=(m // tm, n // tn),
        in_specs=[pl.BlockSpec((tm, kh), lambda i, j: (i, 0)),
                  pl.BlockSpec((tm, kh), lambda i, j: (i, 0)),
                  pl.BlockSpec((kh, tn), lambda i, j: (0, j)),
                  pl.BlockSpec((kh, tn), lambda i, j: (1, j)),
                  pl.BlockSpec((tm, tn), lambda i, j: (i, j))],
        out_specs=pl.BlockSpec((tm, tn), lambda i, j: (i, j)),
        compiler_params=_params(("parallel", "parallel"), vm),
        name="even_out_proj",
    )(a1, a2, w, w, h)


def _ple_kernel(hn_ref, wg_ref, p_ref, wp_ref, h_ref, o_ref):
    gate = jax.nn.sigmoid(jnp.dot(hn_ref[...], wg_ref[...], preferred_element_type=F32))
    proj = jnp.dot(p_ref[...], wp_ref[...], preferred_element_type=F32)
    o_ref[...] = h_ref[...] + gate * proj


def ple(hn, w_gate, p, w_proj, h):
    m, d = hn.shape
    n = w_gate.shape[1]
    pd = p.shape[1]
    tm = _pick(m, (1024, 512, 256, 128, 64, 32, 16, 8))
    tn = _pick(n, (512, 256, 128))
    vm = 2 * (_nbytes((tm, d), BF16) + _nbytes((d, tn), BF16) + _nbytes((tm, pd), BF16) + _nbytes((pd, tn), BF16))
    vm += 6 * _nbytes((tm, tn), F32)
    return pl.pallas_call(
        _ple_kernel,
        out_shape=jax.ShapeDtypeStruct((m, n), F32),
        grid=(m // tm, n // tn),
        in_specs=[pl.BlockSpec((tm, d), lambda i, j: (i, 0)),
                  pl.BlockSpec((d, tn), lambda i, j: (0, j)),
                  pl.BlockSpec((tm, pd), lambda i, j: (i, 0)),
                  pl.BlockSpec((pd, tn), lambda i, j: (0, j)),
                  pl.BlockSpec((tm, tn), lambda i, j: (i, j))],
        out_specs=pl.BlockSpec((tm, tn), lambda i, j: (i, j)),
        compiler_params=_params(("parallel", "parallel"), vm),
        name="ple",
    )(hn, w_gate, p, w_proj, h)


def _bdot(a, b):
    return jnp.dot(a.astype(BF16), b.astype(BF16), preferred_element_type=F32)


def _bdot_nt(a, b):
    return lax.dot_general(a.astype(BF16), b.astype(BF16), (((1,), (1,)), ((), ())),
                           preferred_element_type=F32)


def _bdot_tn(a, b):
    return lax.dot_general(a.astype(BF16), b.astype(BF16), (((0,), (0,)), ((), ())),
                           preferred_element_type=F32)


def _split3(x):
    hi = x.astype(BF16)
    r1 = x - hi.astype(F32)
    mid = r1.astype(BF16)
    lo = (r1 - mid.astype(F32)).astype(BF16)
    return hi, mid, lo


def _bmm(a, b):
    return jnp.einsum("bij,bjk->bik", a.astype(BF16), b.astype(BF16), preferred_element_type=F32)


def _bmm_nt(a, b):
    return jnp.einsum("bik,bjk->bij", a.astype(BF16), b.astype(BF16), preferred_element_type=F32)


def _bmm_tn(a, b):
    return jnp.einsum("bti,btj->bij", a.astype(BF16), b.astype(BF16), preferred_element_type=F32)


def _rwkv_a_kernel(r_ref, k_ref, v_ref, ai_ref, lw_ref, kk_ref, ka_ref, mt_ref, nt_ref, q_ref, o0_ref, *, hb, ncb):
    t = RW_CHUNK
    n = RW_HEAD
    nb = hb * ncb
    row = lax.broadcasted_iota(jnp.int32, (t, t), 0)
    col = lax.broadcasted_iota(jnp.int32, (t, t), 1)
    incl = col <= row
    strict = col < row
    tri = jnp.broadcast_to(incl.astype(BF16), (nb, t, t))
    eye = (row == col).astype(F32)
    drow = lax.broadcasted_iota(jnp.int32, (n, n), 0)
    dcol = lax.broadcasted_iota(jnp.int32, (n, n), 1)

    def units(ref):
        return jnp.stack([ref[c * t:(c + 1) * t, h * n:(h + 1) * n] for c in range(ncb) for h in range(hb)])

    def unit_params(ref):
        return jnp.stack([ref[:, h * n:(h + 1) * n] for c in range(ncb) for h in range(hb)])

    k = units(k_ref)
    ai = units(ai_ref)
    kk = k * unit_params(kk_ref)
    kk = kk * lax.rsqrt(jnp.maximum(jnp.sum(kk * kk, axis=-1, keepdims=True), 1e-24))
    k = k * (1.0 + (ai - 1.0) * unit_params(ka_ref))
    a = -kk
    b = kk * ai
    lw = units(lw_ref)
    hi, mid, lo = _split3(lw)
    cum = (jnp.einsum("bij,bjk->bik", tri, hi, preferred_element_type=F32)
           + jnp.einsum("bij,bjk->bik", tri, mid, preferred_element_type=F32)
           + jnp.einsum("bij,bjk->bik", tri, lo, preferred_element_type=F32))
    cum_t = cum[:, t - 1:t, :]
    e_in = jnp.exp(cum)
    e_out = jnp.exp(-cum)
    e_prev = jnp.exp(cum - lw)
    e_rem = jnp.exp(cum_t - cum)
    p_t = jnp.exp(cum_t)
    r = units(r_ref)
    v = units(v_ref)
    rh = r * e_in
    ah = a * e_prev
    bh = b * e_out
    kh = k * e_out
    bt = b * e_rem
    kt = k * e_rem
    g = _bmm_nt(jnp.concatenate([ah, rh], axis=1), jnp.concatenate([bh, kh], axis=1))
    row2 = lax.broadcasted_iota(jnp.int32, (t, 2 * t), 0)
    col2 = lax.broadcasted_iota(jnp.int32, (t, 2 * t), 1) % t
    n_cat = jnp.where(col2 < row2, g[:, :t, :], 0.0)
    l_cat = jnp.where(col2 <= row2, g[:, t:, :], 0.0)
    n_ab = n_cat[:, :, :t]
    x = eye + n_ab
    pw = n_ab
    for _ in range((t - 1).bit_length() - 1):
        pw = _bmm(pw, pw)
        x = x + _bmm(x, pw)
    zeros = jnp.zeros_like(v)
    akv = _bmm(n_cat, jnp.concatenate([zeros, v], axis=1))
    wu = _bmm(x, jnp.concatenate([ah, akv], axis=2))
    wu_v = jnp.concatenate([wu, jnp.concatenate([zeros, v], axis=2)], axis=1)
    l2 = _bmm(l_cat, wu_v)
    b2 = _bmm_tn(jnp.concatenate([bt, kt], axis=1), wu_v)
    qt = rh + l2[:, :, :n]
    o0 = l2[:, :, n:]
    mt = jnp.where(drow == dcol, p_t, 0.0) + b2[:, :, :n]
    nt = b2[:, :, n:]
    for c in range(ncb):
        rows = slice(c * t, (c + 1) * t)
        q_ref[rows, :] = jnp.concatenate([qt[c * hb + h] for h in range(hb)], axis=-1)
        o0_ref[rows, :] = jnp.concatenate([o0[c * hb + h] for h in range(hb)], axis=-1)
        mt_ref[:, c] = mt[c * hb:(c + 1) * hb]
        nt_ref[:, c] = nt[c * hb:(c + 1) * hb]


def _bdot3(a, b):
    a_hi = a.astype(BF16)
    a_lo = (a - a_hi.astype(F32)).astype(BF16)
    b_hi = b.astype(BF16)
    b_lo = (b - b_hi.astype(F32)).astype(BF16)
    mm = lambda x, y: jnp.einsum("bij,bjk->bik", x, y, preferred_element_type=F32)
    return mm(a_hi, b_hi) + mm(a_hi, b_lo) + mm(a_lo, b_hi)


def _rwkv_b_kernel(mt_ref, nt_ref, q_ref, o0_ref, r_ref, k_ref, v_ref, ai_ref, g_ref, ka_ref, rk_ref, lw_ref, lb_ref,
                   o_ref, z_ref, *, nh):
    n = RW_HEAD
    c = pl.program_id(0)

    @pl.when(c == 0)
    def _():
        z_ref[...] = jnp.zeros_like(z_ref)

    def heads(ref):
        return jnp.stack([ref[:, h * n:(h + 1) * n] for h in range(nh)])

    z = z_ref[...]
    o = heads(o0_ref) + _bdot3(heads(q_ref), z)
    z_ref[...] = nt_ref[:, 0] + _bdot3(mt_ref[:, 0], z)
    mean = jnp.mean(o, axis=-1, keepdims=True)
    var = jnp.mean(jnp.square(o - mean), axis=-1, keepdims=True)
    o = (o - mean) * lax.rsqrt(var + GN_EPS) * heads(lw_ref) + heads(lb_ref)
    k = heads(k_ref) * (1.0 + (heads(ai_ref) - 1.0) * heads(ka_ref))
    bonus = jnp.sum(heads(r_ref) * k * heads(rk_ref), axis=-1, keepdims=True) * heads(v_ref)
    out = (o + bonus) * heads(g_ref)
    o_ref[...] = jnp.concatenate([out[h] for h in range(nh)], axis=-1).astype(o_ref.dtype)


def rwkv_mix(r, k, v, ai, lw, g, k_k, k_a, r_k, lnx_w, lnx_b):
    s, dim = r.shape
    n = RW_HEAD
    nh = dim // n
    t = RW_CHUNK
    nc = s // t
    hb = _pick(nh, (RW_HEADS_PER_STEP, 4, 2))
    ncb = _pick(nc, (RW_CHUNKS_PER_STEP, 2, 1))
    row = lambda p: p.reshape(1, dim).astype(F32)
    seq_spec = pl.BlockSpec((ncb * t, hb * n), lambda g_, c: (c, g_))
    par_spec = pl.BlockSpec((1, hb * n), lambda g_, c: (0, g_))
    mat_spec = pl.BlockSpec((hb, ncb, n, n), lambda g_, c: (g_, c, 0, 0))
    mt, nt, qt, o0 = pl.pallas_call(
        functools.partial(_rwkv_a_kernel, hb=hb, ncb=ncb),
        out_shape=(jax.ShapeDtypeStruct((nh, nc, n, n), F32), jax.ShapeDtypeStruct((nh, nc, n, n), F32),
                   jax.ShapeDtypeStruct((s, dim), F32), jax.ShapeDtypeStruct((s, dim), F32)),
        grid=(nh // hb, nc // ncb),
        in_specs=[seq_spec] * 5 + [par_spec] * 2,
        out_specs=(mat_spec, mat_spec, seq_spec, seq_spec),
        compiler_params=_params(("parallel", "parallel"), 32 << 20),
        name="rwkv_chunk_local",
    )(r, k, v, ai, lw, row(k_k), row(k_a))
    seq_all = pl.BlockSpec((t, dim), lambda c: (c, 0))
    par_all = pl.BlockSpec((1, dim), lambda c: (0, 0))
    mat_all = pl.BlockSpec((nh, 1, n, n), lambda c: (0, c, 0, 0))
    return pl.pallas_call(
        functools.partial(_rwkv_b_kernel, nh=nh),
        out_shape=jax.ShapeDtypeStruct((s, dim), BF16),
        grid=(nc,),
        in_specs=[mat_all, mat_all] + [seq_all] * 7 + [par_all] * 4,
        out_specs=seq_all,
        scratch_shapes=[pltpu.VMEM((nh, n, n), F32)],
        compiler_params=_params(("arbitrary",), 32 << 20),
        name="rwkv_state_scan",
    )(mt, nt, qt, o0, r, k, v, ai, g, row(k_a), row(r_k), row(lnx_w), row(lnx_b))


def _flash_kernel(qn_ref, qp_ref, kn_ref, kp_ref, v_ref, o_ref, q_sc, m_sc, acc_sc, s0_sc, s1_sc, *, t):
    i = pl.program_id(1)
    nt = t // LANES
    q_sc[...] = jnp.concatenate([qn_ref[...], qp_ref[...]], axis=1)
    m_sc[...] = jnp.full_like(m_sc, -1e30)
    acc_sc[...] = jnp.zeros_like(acc_sc)
    ones = jnp.ones((t, LANES), BF16)

    def scores(j, s_ref):
        start = pl.multiple_of(j * t, t)
        kb = jnp.concatenate([kn_ref[pl.ds(start, t), :], kp_ref[pl.ds(start, t), :]], axis=1)
        s_ref[...] = lax.dot_general(q_sc[...], kb, (((1,), (1,)), ((), ())), preferred_element_type=F32)

    def softmax_pv(j, s_ref, masked):
        start = pl.multiple_of(j * t, t)
        vb = jnp.concatenate([v_ref[pl.ds(start, t), :], ones], axis=1)
        tiles = [s_ref[:, c * LANES:(c + 1) * LANES] for c in range(nt)]
        if masked:
            rows = lax.broadcasted_iota(jnp.int32, (t, LANES), 0)
            cols = lax.broadcasted_iota(jnp.int32, (t, LANES), 1)
            tiles = [jnp.where(cols + c * LANES <= rows, tl, -1e30) for c, tl in enumerate(tiles)]
        mx = tiles[0]
        for c in range(1, nt):
            mx = jnp.maximum(mx, tiles[c])
        m_prev = m_sc[...]
        m_new = jnp.maximum(m_prev, jnp.max(mx, axis=1, keepdims=True))
        alpha = jnp.exp2(m_prev - m_new)
        p = jnp.concatenate([jnp.exp2(tl - m_new).astype(BF16) for tl in tiles], axis=1)
        pv = jnp.dot(p, vb, preferred_element_type=F32)
        acc_sc[...] = jnp.concatenate([alpha, alpha], axis=1) * acc_sc[...] + pv
        m_sc[...] = m_new

    def stage(j, s_cur, s_next):
        scores(j + 1, s_next)
        softmax_pv(j, s_cur, False)

    scores(0, s0_sc)

    def pair(jp, carry):
        stage(2 * jp, s0_sc, s1_sc)
        stage(2 * jp + 1, s1_sc, s0_sc)
        return carry

    lax.fori_loop(0, i // 2, pair, 0)

    @pl.when(i % 2 == 1)
    def _():
        stage(i - 1, s0_sc, s1_sc)
        softmax_pv(i, s1_sc, True)

    @pl.when(i % 2 == 0)
    def _():
        softmax_pv(i, s0_sc, True)

    o_ref[...] = (acc_sc[:, :LANES] / acc_sc[:, LANES:]).astype(o_ref.dtype)


def flash_attention(q_nope, q_pe, kv, k_pe, nh):
    s = q_nope.shape[0]
    dh = LANES
    tq = _pick(s, (FLASH_T, 512, 256, 128))
    vm = 2 * (3 * _nbytes((s, dh), BF16) + 3 * _nbytes((tq, dh), BF16))
    vm += 5 * _nbytes((tq, tq), F32) + 4 * _nbytes((tq, 2 * dh), F32)
    scratch = [pltpu.VMEM((tq, 2 * dh), BF16), pltpu.VMEM((tq, dh), F32), pltpu.VMEM((tq, 2 * dh), F32),
               pltpu.VMEM((tq, tq), F32), pltpu.VMEM((tq, tq), F32)]
    return pl.pallas_call(
        functools.partial(_flash_kernel, t=tq),
        out_shape=jax.ShapeDtypeStruct((s, nh * dh), BF16),
        grid=(nh, s // tq),
        in_specs=[pl.BlockSpec((tq, dh), lambda h, i: (i, h)),
                  pl.BlockSpec((tq, dh), lambda h, i: (i, h)),
                  pl.BlockSpec((s, dh), lambda h, i: (0, h)),
                  pl.BlockSpec((s, dh), lambda h, i: (0, 0)),
                  pl.BlockSpec((s, dh), lambda h, i: (0, nh + h))],
        out_specs=pl.BlockSpec((tq, dh), lambda h, i: (i, h)),
        scratch_shapes=scratch,
        compiler_params=_params(("parallel", "arbitrary"), vm),
        name="mla_flash_attention",
    )(q_nope, q_pe, kv, k_pe, kv)


def _spatial_kernel(w_ref, bias_ref, v_ref, u_ref, o_ref, *, nch):
    c = CHUNK
    row = lax.broadcasted_iota(jnp.int32, (c, c), 0)
    col = lax.broadcasted_iota(jnp.int32, (c, c), 1)
    w = jnp.where(col <= row, w_ref[0], 0.0).astype(BF16)
    bias = bias_ref[0]
    for ci in range(nch):
        sl = pl.ds(ci * c, c)
        sg = jnp.dot(w, v_ref[sl, :], preferred_element_type=F32) + bias
        o_ref[sl, :] = (u_ref[sl, :].astype(F32) * sg).astype(o_ref.dtype)


def spatial_gate(z, vn, w_sp, b_sp):
    s, d = vn.shape
    g = w_sp.shape[0]
    gd = d // g
    nch = _pick(s // CHUNK, (16, 8, 4, 2, 1))
    rows = nch * CHUNK
    bias = jnp.broadcast_to(b_sp.astype(F32)[:, :, None], (g, CHUNK, gd))
    return pl.pallas_call(
        functools.partial(_spatial_kernel, nch=nch),
        out_shape=jax.ShapeDtypeStruct((s, d), BF16),
        grid=(g, s // rows),
        in_specs=[pl.BlockSpec((1, CHUNK, CHUNK), lambda gi, ci: (gi, 0, 0)),
                  pl.BlockSpec((1, CHUNK, gd), lambda gi, ci: (gi, 0, 0)),
                  pl.BlockSpec((rows, gd), lambda gi, ci: (ci, gi)),
                  pl.BlockSpec((rows, gd), lambda gi, ci: (ci, gi))],
        out_specs=pl.BlockSpec((rows, gd), lambda gi, ci: (ci, gi)),
        compiler_params=_params(("parallel", "parallel"), 16 << 20),
        name="spatial_gate",
    )(w_sp.astype(F32), bias, vn, z)


def _norm_router_kernel(h_ref, g_ref, wr_ref, xn_ref, lg_ref, *, eps):
    x = h_ref[...]
    ms = jnp.mean(x * x, axis=-1, keepdims=True)
    xn = x * lax.rsqrt(ms + eps) * g_ref[...]
    x_hi = xn.astype(BF16)
    half = xn.shape[1] // 2
    rounded = x_hi.astype(F32)
    lo_bits = pltpu.bitcast(rounded[:, :half], jnp.uint32) >> 16
    hi_bits = pltpu.bitcast(rounded[:, half:], jnp.uint32) & jnp.uint32(0xFFFF0000)
    xn_ref[...] = lo_bits | hi_bits
    x_lo = (xn - x_hi.astype(F32)).astype(BF16)
    w = wr_ref[...]
    w_hi = w.astype(BF16)
    w_lo = (w - w_hi.astype(F32)).astype(BF16)
    lg_ref[...] = (jnp.dot(x_hi, w_hi, preferred_element_type=F32) + jnp.dot(x_hi, w_lo, preferred_element_type=F32)
                   + jnp.dot(x_lo, w_hi, preferred_element_type=F32))


def norm_router(h, g, w_router):
    m, d = h.shape
    e = w_router.shape[1]
    wr = jnp.pad(w_router.astype(F32), ((0, 0), (0, LANES - e)))
    tm = _pick(m, (512, 256, 128, 64, 32, 16, 8))
    vm = 2 * (2 * _nbytes((tm, d), F32) + _nbytes((d, LANES), F32)) + 6 * _nbytes((tm, d), F32)
    xn, lg = pl.pallas_call(
        functools.partial(_norm_router_kernel, eps=RMS_EPS),
        out_shape=(jax.ShapeDtypeStruct((m, d // 2), jnp.uint32), jax.ShapeDtypeStruct((m, LANES), F32)),
        grid=(m // tm,),
        in_specs=[pl.BlockSpec((tm, d), lambda i: (i, 0)),
                  pl.BlockSpec((1, d), lambda i: (0, 0)),
                  pl.BlockSpec((d, LANES), lambda i: (0, 0))],
        out_specs=(pl.BlockSpec((tm, d // 2), lambda i: (i, 0)), pl.BlockSpec((tm, LANES), lambda i: (i, 0))),
        compiler_params=_params(("parallel",), vm),
        name="norm_router",
    )(h, g.reshape(1, d).astype(F32), wr)
    return xn, lg[:, :e]


def _row_copy(src_hbm, src_row, dst, dst_row, sem):
    return pltpu.make_async_copy(src_hbm.at[pl.ds(src_row, 1)], dst.at[pl.ds(dst_row, 1)], sem)


def _moe_up_kernel(te_ref, tv_ref, src_ref, x_hbm, wg_ref, wu_ref, gate_ref, o_ref, xbuf, xb16, sems, *, tile):
    t = pl.program_id(0)
    n_tiles = pl.num_programs(0)
    slot = t % 2

    def start_tile(tt, sl):
        def body(r, carry):
            _row_copy(x_hbm, src_ref[tt * tile + r], xbuf.at[sl], r, sems.at[sl]).start()
            return carry
        lax.fori_loop(0, tile, body, 0)

    def wait_tile(sl):
        def body(r, carry):
            _row_copy(x_hbm, 0, xbuf.at[sl], r, sems.at[sl]).wait()
            return carry
        lax.fori_loop(0, tile, body, 0)

    @pl.when(pl.program_id(1) == 0)
    def _():
        @pl.when(t == 0)
        def _():
            start_tile(0, 0)

        wait_tile(slot)

        @pl.when(t + 1 < n_tiles)
        def _():
            start_tile(t + 1, 1 - slot)

        words = xbuf[slot]
        lo = pltpu.bitcast(words << 16, F32)
        hi = pltpu.bitcast(words & jnp.uint32(0xFFFF0000), F32)
        xb16[...] = jnp.concatenate([lo, hi], axis=1).astype(BF16)

    @pl.when(tv_ref[t] == 1)
    def _():
        x = xb16[...]
        g = jnp.dot(x, wg_ref[...], preferred_element_type=F32)
        u = jnp.dot(x, wu_ref[...], preferred_element_type=F32)
        o_ref[...] = (gate_ref[...] * (g * jax.nn.sigmoid(g) * u)).astype(o_ref.dtype)

    @pl.when(tv_ref[t] == 0)
    def _():
        o_ref[...] = jnp.zeros_like(o_ref)


def _moe_down_kernel(te_ref, tv_ref, a_ref, w_ref, o_ref):
    t = pl.program_id(0)

    @pl.when(tv_ref[t] == 1)
    def _():
        o_ref[...] = jnp.dot(a_ref[...], w_ref[...], preferred_element_type=F32)

    @pl.when(tv_ref[t] == 0)
    def _():
        o_ref[...] = jnp.zeros_like(o_ref)


def _moe_combine_kernel(s1_ref, s2_ref, h_ref, g_ref, d_hbm, o_ref, on_ref, buf, sem, *, tm):
    base = pl.program_id(0) * tm

    def start(r, carry):
        _row_copy(d_hbm, s1_ref[base + r], buf.at[0], r, sem).start()
        _row_copy(d_hbm, s2_ref[base + r], buf.at[1], r, sem).start()
        return carry

    lax.fori_loop(0, tm, start, 0)

    def wait(r, carry):
        _row_copy(d_hbm, 0, buf.at[0], r, sem).wait()
        _row_copy(d_hbm, 0, buf.at[1], r, sem).wait()
        return carry

    lax.fori_loop(0, tm, wait, 0)
    o = h_ref[...] + buf[0] + buf[1]
    o_ref[...] = o
    ms = jnp.mean(o * o, axis=-1, keepdims=True)
    on_ref[...] = (o * lax.rsqrt(ms + RMS_EPS) * g_ref[...]).astype(on_ref.dtype)


def _route(logits, tile):
    s, ne = logits.shape
    top_val, top_idx = lax.top_k(logits, TOP_K)
    top_w = jax.nn.softmax(top_val, axis=-1)
    e_flat = top_idx.reshape(-1).astype(jnp.int32)
    onehot = (e_flat[:, None] == jnp.arange(ne, dtype=jnp.int32)[None, :]).astype(jnp.int32)
    csum = jnp.cumsum(onehot, axis=0)
    rank = jnp.sum(csum * onehot, axis=1) - 1
    counts = csum[-1]
    padded = (counts + tile - 1) // tile * tile
    ends = jnp.cumsum(padded)
    slot = (ends - padded)[e_flat] + rank
    p_rows = s * TOP_K + ne * tile
    tok = (jnp.arange(s * TOP_K, dtype=jnp.int32) // TOP_K).astype(F32)
    placed = jnp.zeros((p_rows, 2), F32).at[slot].set(jnp.stack([tok, top_w.reshape(-1)], axis=1))
    src = placed[:, 0].astype(jnp.int32)
    gate = placed[:, 1]
    tstart = jnp.arange(p_rows // tile, dtype=jnp.int32) * tile
    tile_e = jnp.minimum(jnp.sum((tstart[:, None] >= ends[None, :]).astype(jnp.int32), axis=1), ne - 1)
    tile_valid = (tstart < ends[-1]).astype(jnp.int32)
    return src, gate.reshape(p_rows, 1), tile_e, tile_valid, slot.reshape(s, TOP_K)


def _rw_prep_kernel(yr_ref, yk_ref, yv_ref, yl_ref, pr_ref, pk_ref, pv_ref, pl_ref,
                    mur_ref, muk_ref, muv_ref, mul_ref, w2_ref, a2_ref, g2_ref, w0_ref, a0_ref,
                    r_ref, k_ref, v_ref, lw_ref, ai_ref, g_ref):
    first = pl.program_id(0) == 0

    def shifted(x_ref, p_ref, mu_ref):
        x = x_ref[...].astype(F32)
        prev = jnp.where(first, 0.0, p_ref[...].astype(F32)[7:8, :])
        row = lax.broadcasted_iota(jnp.int32, x.shape, 0)
        xp = jnp.where(row == 0, prev, pltpu.roll(x, 1, axis=0))
        return x + (xp - x) * mu_ref[...]

    r_ref[...] = shifted(yr_ref, pr_ref, mur_ref)
    k_ref[...] = shifted(yk_ref, pk_ref, muk_ref)
    v_ref[...] = shifted(yv_ref, pv_ref, muv_ref)
    lo = shifted(yl_ref, pl_ref, mul_ref)
    lane = lax.broadcasted_iota(jnp.int32, lo.shape, 1)
    act = jnp.where(lane < DECAY_LORA, jnp.tanh(lo),
                    jnp.where(lane < DECAY_LORA + AAA_LORA, lo,
                              jnp.where(lane < DECAY_LORA + AAA_LORA + GATE_LORA, jax.nn.sigmoid(lo), 0.0)))
    act = act.astype(BF16)
    w_pre = w0_ref[...] + jnp.dot(act, w2_ref[...], preferred_element_type=F32)
    x = -w_pre
    softplus = jnp.maximum(x, 0.0) + jnp.log(1.0 + jnp.exp(-jnp.abs(x)))
    lw_ref[...] = -jnp.exp(-softplus - 0.5)
    ai_ref[...] = jax.nn.sigmoid(a0_ref[...] + jnp.dot(act, a2_ref[...], preferred_element_type=F32))
    g_ref[...] = jnp.dot(act, g2_ref[...], preferred_element_type=F32)


def _rwkv_group(y, col_lo, mu, w0, w2, a0, a2, g2, k_k, k_a, r_k, lnx_w, lnx_b):
    s = y.shape[0]
    dim = RW_DIM
    lora = DECAY_LORA + AAA_LORA + GATE_LORA
    lwid = -(-lora // LANES) * LANES
    assert col_lo % lwid == 0 and dim % LANES == 0
    tm = _pick(s, (128, 64, 32, 16, 8))
    pb = tm // 8
    mu_rkv = mu[:3 * dim].reshape(1, 3 * dim)
    mu_lo = jnp.pad(mu[3 * dim:], (0, lwid - lora)).reshape(1, lwid)

    def pad_rows(w, start):
        return jnp.pad(w, ((start, lwid - start - w.shape[0]), (0, 0))).astype(BF16)

    y_spec = lambda c: pl.BlockSpec((tm, dim), lambda i: (i, c))
    p_spec = lambda c: pl.BlockSpec((8, dim), lambda i: (jnp.maximum(i * pb - 1, 0), c))
    mu_spec = lambda c: pl.BlockSpec((1, dim), lambda i: (0, c))
    cl = col_lo // lwid
    w_spec = pl.BlockSpec((lwid, dim), lambda i: (0, 0))
    row_spec = pl.BlockSpec((1, dim), lambda i: (0, 0))
    out_spec = pl.BlockSpec((tm, dim), lambda i: (i, 0))
    out_sds = jax.ShapeDtypeStruct((s, dim), F32)
    vm = 2 * (3 * _nbytes((tm, dim), BF16) + 6 * _nbytes((tm, dim), F32) + 3 * _nbytes((lwid, dim), BF16))
    vm += 6 * _nbytes((tm, dim), F32)
    r, k, v, lw, ai, g = pl.pallas_call(
        _rw_prep_kernel,
        out_shape=(out_sds,) * 6,
        grid=(s // tm,),
        in_specs=[y_spec(0), y_spec(1), y_spec(2), pl.BlockSpec((tm, lwid), lambda i: (i, cl)),
                  p_spec(0), p_spec(1), p_spec(2),
                  pl.BlockSpec((8, lwid), lambda i: (jnp.maximum(i * pb - 1, 0), cl)),
                  mu_spec(0), mu_spec(1), mu_spec(2), pl.BlockSpec((1, lwid), lambda i: (0, 0)),
                  w_spec, w_spec, w_spec, row_spec, row_spec],
        out_specs=(out_spec,) * 6,
        compiler_params=_params(("parallel",), vm),
        name="rwkv_prep",
    )(y, y, y, y, y, y, y, y, mu_rkv, mu_rkv, mu_rkv, mu_lo,
      pad_rows(w2, 0), pad_rows(a2, DECAY_LORA), pad_rows(g2, DECAY_LORA + AAA_LORA),
      w0.reshape(1, dim), a0.reshape(1, dim))
    return rwkv_mix(r, k, v, ai, lw, g, k_k, k_a, r_k.reshape(-1), lnx_w, lnx_b)


def _rope(x, cos, sin):
    half = x.shape[-1] // 2
    x1, x2 = x[..., :half], x[..., half:]
    return jnp.concatenate([x1 * cos - x2 * sin, x2 * cos + x1 * sin], axis=-1)


def _mla_group(y, col_q, col_kv, col_pe, cos, sin, q_norm, w_uq, kv_norm, w_ukv):
    s = y.shape[0]
    nh = MLA_HEADS
    cqn = rmsnorm(y, q_norm, BF16, width=Q_LORA, col_block=col_q // Q_LORA)
    ckvn = rmsnorm(y, kv_norm, BF16, width=KV_LORA, col_block=col_kv // KV_LORA)
    assert QK_NOPE == LANES and V_HEAD == LANES and QK_ROPE <= LANES
    half = QK_ROPE // 2
    scale = (QK_NOPE + QK_ROPE) ** -0.5 * 1.4426950408889634
    wq = w_uq.reshape(Q_LORA, nh, QK_NOPE + QK_ROPE)
    w_nope = wq[:, :, :QK_NOPE].reshape(Q_LORA, nh * LANES).astype(BF16)
    w_pe = wq[:, :, QK_NOPE:]
    w_rot = jnp.concatenate([-w_pe[..., half:], w_pe[..., :half]], axis=-1)
    lane_pad = ((0, 0), (0, 0), (0, LANES - QK_ROPE))
    w_pe = jnp.pad(w_pe, lane_pad).reshape(Q_LORA, nh * LANES).astype(BF16)
    w_rot = jnp.pad(w_rot, lane_pad).reshape(Q_LORA, nh * LANES).astype(BF16)
    tm = _pick(s, (1024, 512, 256, 128, 64, 32, 16, 8))
    tn = _pick(nh * LANES, (512, 256, 128))
    table_pad = ((0, 0), (0, LANES - QK_ROPE))
    cos_t = jnp.tile(jnp.pad(jnp.concatenate([cos, cos], axis=-1), table_pad), (1, tn // LANES))
    sin_t = jnp.tile(jnp.pad(jnp.concatenate([sin, sin], axis=-1), table_pad), (1, tn // LANES))
    q_nope = matmul(cqn, [w_nope], lambda accs, extras: accs[0] * scale, BF16, tm=tm, tn=tn, name="mla_q_nope_up")
    q_pe = matmul(cqn, [w_pe, w_rot], lambda accs, extras: (accs[0] * extras[0] + accs[1] * extras[1]) * scale,
                  BF16, tm=tm, tn=tn,
                  extras=[(cos_t, (tm, tn), lambda i, jn: (i, 0)), (sin_t, (tm, tn), lambda i, jn: (i, 0))],
                  name="mla_q_rope_up")
    w_kv = jnp.transpose(w_ukv.reshape(KV_LORA, nh, 2, LANES), (0, 2, 1, 3)).reshape(KV_LORA, 2 * nh * LANES)
    kv = matmul(ckvn, [w_kv.astype(BF16)], _epi_id, BF16, tm=tm, tn=tn, name="mla_kv_up")
    k_pe = _rope(y[:, col_pe:col_pe + QK_ROPE].astype(F32), cos, sin)
    k_pe = jnp.pad(k_pe, table_pad).astype(BF16)
    return flash_attention(q_nope, q_pe, kv, k_pe, nh)


def _even_layer(h, hn, j, cos, sin, prm):
    w_in = prm["ev_in_proj"][j]
    rw_end = 3 * RW_DIM
    lora_w = DECAY_LORA + AAA_LORA + GATE_LORA
    rw_proj = rw_end + lora_w
    w_perm = jnp.concatenate([
        w_in[:, :rw_end],
        w_in[:, rw_proj:rw_proj + Q_LORA + KV_LORA],
        w_in[:, rw_end:rw_proj],
        w_in[:, rw_proj + Q_LORA + KV_LORA:],
    ], axis=1).astype(BF16)
    y = matmul(hn, [w_perm], _epi_id, BF16, tm=1024, tn=512, name="even_in_proj")
    col_q = rw_end
    col_kv = rw_end + Q_LORA
    col_lo = rw_end + Q_LORA + KV_LORA
    col_pe = col_lo + lora_w
    mu = prm["rw_mu"][j]
    o_rw = _rwkv_group(y, col_lo, mu, prm["rw_w0"][j], prm["rw_w2"][j], prm["rw_a0"][j], prm["rw_a2"][j],
                       prm["rw_g2"][j], prm["rw_k_k"][j], prm["rw_k_a"][j], prm["rw_r_k"][j],
                       prm["rw_lnx_w"][j], prm["rw_lnx_b"][j])
    o_mla = _mla_group(y, col_q, col_kv, col_pe, cos, sin, prm["mla_q_norm"][j], prm["mla_w_uq"][j],
                       prm["mla_kv_norm"][j], prm["mla_w_ukv"][j])
    return mix_out_proj(o_rw, o_mla, prm["ev_out_proj"][j].astype(BF16), h)


def _dense_ffn(h, hn, wg, wu, wd):
    d, f = wg.shape
    fp = -(-f // 1024) * 1024 if f > 1024 else f
    pad = fp - f
    wg = jnp.pad(wg.astype(BF16), ((0, 0), (0, pad)))
    wu = jnp.pad(wu.astype(BF16), ((0, 0), (0, pad)))
    wd = jnp.pad(wd.astype(BF16), ((0, pad), (0, 0)))
    mid = matmul(hn, [wg, wu], _epi_swiglu, BF16, tm=1024, tn=512, name="ffn_up")
    tm = _pick(h.shape[0], (1024, 512, 256, 128, 64, 32, 16, 8))
    tn = _pick(h.shape[1], (1024, 512, 256, 128))
    tk = _pick(fp, (2816, 2048, 1024, 512, 256, 128))
    nk = fp // tk
    return matmul(mid, [wd], _epi_residual, F32, tm=tm, tn=tn, tk=tk,
                  extras=[_res_extra(h, tm, tn, nk)], name="ffn_down")


def _moe(h, g_norm, w_router, wg, wu, wd, g_next):
    ne, d, f = wg.shape
    s = h.shape[0]
    xn, logits = norm_router(h, g_norm, w_router)
    tile = _pick(s, (MOE_TILE, 256, 128, 64, 32, 16, 8))
    src, gate, tile_e, tile_valid, slots = _route(logits, tile)
    p_rows = src.shape[0]
    n_tiles = p_rows // tile
    tn = _pick(f, (512, 256, 128))
    vm = 2 * (2 * _nbytes((d, tn), BF16) + _nbytes((tile, tn), BF16))
    vm += 2 * _nbytes((tile, d), F32) + 2 * _nbytes((tile, d), BF16) + 4 * _nbytes((tile, tn), F32)
    w_spec = pl.BlockSpec((None, d, tn), lambda t, jn, te, tv, sr: (te[t], 0, jn))
    mid = pl.pallas_call(
        functools.partial(_moe_up_kernel, tile=tile),
        out_shape=jax.ShapeDtypeStruct((p_rows, f), BF16),
        grid_spec=pltpu.PrefetchScalarGridSpec(
            num_scalar_prefetch=3, grid=(n_tiles, f // tn),
            in_specs=[pl.BlockSpec(memory_space=pl.ANY), w_spec, w_spec,
                      pl.BlockSpec((tile, 1), lambda t, jn, te, tv, sr: (t, 0))],
            out_specs=pl.BlockSpec((tile, tn), lambda t, jn, te, tv, sr: (t, jn)),
            scratch_shapes=[pltpu.VMEM((2, tile, d // 2), jnp.uint32), pltpu.VMEM((tile, d), BF16),
                            pltpu.SemaphoreType.DMA((2,))]),
        compiler_params=_params(("arbitrary", "arbitrary"), vm),
        name="moe_up",
    )(tile_e, tile_valid, src, xn, wg.astype(BF16), wu.astype(BF16), gate)
    tn2 = _pick(d, (1024, 512, 256, 128))
    vm = 2 * (_nbytes((tile, f), BF16) + _nbytes((f, tn2), BF16) + _nbytes((tile, tn2), F32)) + _nbytes((tile, tn2), F32)
    dn = pl.pallas_call(
        _moe_down_kernel,
        out_shape=jax.ShapeDtypeStruct((p_rows, d), F32),
        grid_spec=pltpu.PrefetchScalarGridSpec(
            num_scalar_prefetch=2, grid=(n_tiles, d // tn2),
            in_specs=[pl.BlockSpec((tile, f), lambda t, jn, te, tv: (t, 0)),
                      pl.BlockSpec((None, f, tn2), lambda t, jn, te, tv: (te[t], 0, jn))],
            out_specs=pl.BlockSpec((tile, tn2), lambda t, jn, te, tv: (t, jn))),
        compiler_params=_params(("parallel", "parallel"), vm),
        name="moe_down",
    )(tile_e, tile_valid, mid, wd.astype(BF16))
    tm = _pick(s, (256, 128, 64, 32, 16, 8))
    vm = 4 * _nbytes((tm, d), F32) + 2 * _nbytes((tm, d), F32) + 2 * _nbytes((tm, d), F32)
    return pl.pallas_call(
        functools.partial(_moe_combine_kernel, tm=tm),
        out_shape=(jax.ShapeDtypeStruct((s, d), F32), jax.ShapeDtypeStruct((s, d), BF16)),
        grid_spec=pltpu.PrefetchScalarGridSpec(
            num_scalar_prefetch=2, grid=(s // tm,),
            in_specs=[pl.BlockSpec((tm, d), lambda i, s1, s2: (i, 0)), pl.BlockSpec((1, d), lambda i, s1, s2: (0, 0)),
                      pl.BlockSpec(memory_space=pl.ANY)],
            out_specs=(pl.BlockSpec((tm, d), lambda i, s1, s2: (i, 0)), pl.BlockSpec((tm, d), lambda i, s1, s2: (i, 0))),
            scratch_shapes=[pltpu.VMEM((2, tm, d), F32), pltpu.SemaphoreType.DMA(())]),
        compiler_params=_params(("arbitrary",), vm),
        name="moe_combine",
    )(slots[:, 0], slots[:, 1], h, g_next.reshape(1, d).astype(F32), dn)


def _odd_layer(h, hn, j, prm):
    z = matmul(hn, [prm["sg_in_proj"][j].astype(BF16)], _epi_gelu, BF16, tm=1024, tn=512, name="sg_in_proj")
    d = h.shape[1]
    vn = layernorm(z, prm["sg_ln_w"][j], prm["sg_ln_b"][j], BF16, width=d, col_block=1)
    gated = spatial_gate(z, vn, prm["sg_w_spatial"][j], prm["sg_b_spatial"][j])
    tm = _pick(h.shape[0], (1024, 512, 256, 128, 64, 32, 16, 8))
    tn = _pick(d, (512, 256, 128))
    return matmul(gated, [prm["sg_out_proj"][j].astype(BF16)], _epi_residual, F32, tm=tm, tn=tn,
                  extras=[_res_extra(h, tm, tn, 1)], name="sg_out_proj")


def kernel(x, p, positions, norm_mix, norm_ffn, norm_ple, final_norm, ev_in_proj, rw_mu, rw_w0, rw_w2, rw_a0, rw_a2, rw_g2, rw_k_k, rw_k_a, rw_r_k, rw_lnx_w, rw_lnx_b, mla_q_norm, mla_w_uq, mla_kv_norm, mla_w_ukv, ev_out_proj, ffn_w_gate, ffn_w_up, ffn_w_down, sg_in_proj, sg_ln_w, sg_ln_b, sg_w_spatial, sg_b_spatial, sg_out_proj, moe_router, moe_w_gate, moe_w_up, moe_w_down, ple_proj, ple_gate):
    prm = dict(ev_in_proj=ev_in_proj, rw_mu=rw_mu, rw_w0=rw_w0, rw_w2=rw_w2, rw_a0=rw_a0, rw_a2=rw_a2, rw_g2=rw_g2,
               rw_k_k=rw_k_k, rw_k_a=rw_k_a, rw_r_k=rw_r_k, rw_lnx_w=rw_lnx_w, rw_lnx_b=rw_lnx_b,
               mla_q_norm=mla_q_norm, mla_w_uq=mla_w_uq, mla_kv_norm=mla_kv_norm, mla_w_ukv=mla_w_ukv,
               ev_out_proj=ev_out_proj, sg_in_proj=sg_in_proj, sg_ln_w=sg_ln_w, sg_ln_b=sg_ln_b,
               sg_w_spatial=sg_w_spatial, sg_b_spatial=sg_b_spatial, sg_out_proj=sg_out_proj)
    bsz, s, d = x.shape
    assert bsz == 1
    inv = 1.0 / (ROPE_THETA ** (jnp.arange(0, QK_ROPE, 2, dtype=F32) / QK_ROPE))
    ang = positions[0].astype(F32)[:, None] * inv
    cos, sin = jnp.cos(ang), jnp.sin(ang)
    h = x[0]
    for i in range(DEPTH):
        j = i // 2
        if i % 2 == 0:
            hn = rmsnorm(h, norm_mix[i], BF16)
            h = _even_layer(h, hn, j, cos, sin, prm)
            hn = rmsnorm(h, norm_ffn[i], BF16)
            h = _dense_ffn(h, hn, ffn_w_gate[j], ffn_w_up[j], ffn_w_down[j])
        else:
            hn = rmsnorm(h, norm_mix[i], BF16)
            h = _odd_layer(h, hn, j, prm)
            h, hn = _moe(h, norm_ffn[i], moe_router[j], moe_w_gate[j], moe_w_up[j], moe_w_down[j], norm_ple[i])
        if i % 2 == 0:
            hn = rmsnorm(h, norm_ple[i], BF16)
        h = ple(hn, ple_gate[i].astype(BF16), p[i, 0].astype(BF16), ple_proj[i].astype(BF16), h)
    return rmsnorm(h, final_norm, F32)[None]
```

```python
import functools

import jax
import jax.numpy as jnp
from jax import lax
from jax.experimental import pallas as pl
from jax.experimental.pallas import tpu as pltpu

D_MODEL = 4096
DEPTH = 4
RW_HEADS = 32
RW_HEAD = 64
RW_DIM = RW_HEADS * RW_HEAD
DECAY_LORA = 96
AAA_LORA = 96
GATE_LORA = 256
GN_EPS = 64e-5
MLA_HEADS = 16
Q_LORA = 1536
KV_LORA = 512
QK_NOPE = 128
QK_ROPE = 64
V_HEAD = 128
ROPE_THETA = 10000.0
CHUNK = 128
SG_GROUPS = 32
LN_EPS = 1e-5
D_FF = 11008
N_EXPERTS = 8
TOP_K = 2
D_FF_EXPERT = 1536
PLE_DIM = 256
RMS_EPS = 1e-6

V7X_VMEM_BYTES = 64 * 1024 * 1024
LANES = 128
RW_CHUNK = 64
RW_HEADS_PER_STEP = 8
RW_CHUNKS_PER_STEP = 4
FLASH_T = 1024
MOE_TILE = 1024

F32 = jnp.float32
BF16 = jnp.bfloat16


def _pick(dim, candidates):
    for c in candidates:
        if dim % c == 0:
            return c
    return dim


def _params(sem, vmem_bytes, flags=None):
    limit = int(min(V7X_VMEM_BYTES - 6 * 1024 * 1024, max(vmem_bytes * 5 // 4 + (2 << 20), 16 << 20)))
    return pltpu.CompilerParams(dimension_semantics=sem, vmem_limit_bytes=limit, flags=flags)


def _nbytes(shape, dtype):
    n = 1
    for s in shape:
        n *= s
    return n * jnp.dtype(dtype).itemsize


def _rmsnorm_kernel(x_ref, g_ref, o_ref, *, eps):
    x = x_ref[...].astype(F32)
    ms = jnp.mean(x * x, axis=-1, keepdims=True)
    o_ref[...] = (x * lax.rsqrt(ms + eps) * g_ref[...]).astype(o_ref.dtype)


def rmsnorm(x, g, out_dtype, *, width=None, col_block=0):
    m = x.shape[0]
    d = x.shape[1] if width is None else width
    tm = _pick(m, (512, 256, 128, 64, 32, 16, 8))
    vm = 2 * (_nbytes((tm, d), x.dtype) + _nbytes((tm, d), out_dtype)) + 4 * _nbytes((tm, d), F32)
    return pl.pallas_call(
        functools.partial(_rmsnorm_kernel, eps=RMS_EPS),
        out_shape=jax.ShapeDtypeStruct((m, d), out_dtype),
        grid=(m // tm,),
        in_specs=[pl.BlockSpec((tm, d), lambda i: (i, col_block)),
                  pl.BlockSpec((1, d), lambda i: (0, 0))],
        out_specs=pl.BlockSpec((tm, d), lambda i: (i, 0)),
        compiler_params=_params(("parallel",), vm),
        name="rmsnorm",
    )(x, g.reshape(1, d).astype(F32))


def _layernorm_kernel(x_ref, w_ref, b_ref, o_ref, *, eps):
    x = x_ref[...].astype(F32)
    mean = jnp.mean(x, axis=-1, keepdims=True)
    var = jnp.mean(jnp.square(x - mean), axis=-1, keepdims=True)
    o_ref[...] = ((x - mean) * lax.rsqrt(var + eps) * w_ref[...] + b_ref[...]).astype(o_ref.dtype)


def layernorm(x, w, b, out_dtype, *, width, col_block):
    m = x.shape[0]
    tm = _pick(m, (512, 256, 128, 64, 32, 16, 8))
    vm = 2 * (_nbytes((tm, width), x.dtype) + _nbytes((tm, width), out_dtype)) + 4 * _nbytes((tm, width), F32)
    return pl.pallas_call(
        functools.partial(_layernorm_kernel, eps=LN_EPS),
        out_shape=jax.ShapeDtypeStruct((m, width), out_dtype),
        grid=(m // tm,),
        in_specs=[pl.BlockSpec((tm, width), lambda i: (i, col_block)),
                  pl.BlockSpec((1, width), lambda i: (0, 0)),
                  pl.BlockSpec((1, width), lambda i: (0, 0))],
        out_specs=pl.BlockSpec((tm, width), lambda i: (i, 0)),
        compiler_params=_params(("parallel",), vm),
        name="layernorm",
    )(x, w.reshape(1, width).astype(F32), b.reshape(1, width).astype(F32))


def _mm_kernel(*refs, nk, n_w, n_extra, epi):
    a_ref = refs[0]
    w_refs = refs[1:1 + n_w]
    extra_refs = refs[1 + n_w:1 + n_w + n_extra]
    o_ref = refs[1 + n_w + n_extra]
    acc_refs = refs[2 + n_w + n_extra:]
    a = a_ref[...]
    if nk == 1:
        accs = [jnp.dot(a, w[...], preferred_element_type=F32) for w in w_refs]
        o_ref[...] = epi(accs, [e[...] for e in extra_refs]).astype(o_ref.dtype)
        return
    k = pl.program_id(2)

    @pl.when(k == 0)
    def _():
        for acc in acc_refs:
            acc[...] = jnp.zeros_like(acc)

    for acc, w in zip(acc_refs, w_refs):
        acc[...] += jnp.dot(a, w[...], preferred_element_type=F32)

    @pl.when(k == nk - 1)
    def _():
        o_ref[...] = epi([acc[...] for acc in acc_refs], [e[...] for e in extra_refs]).astype(o_ref.dtype)


def matmul(a, ws, epi, out_dtype, *, extras=(), tm=1024, tn=512, tk=None, a_col_block=0, a_width=None,
           w_specs=None, n_out=None, name="matmul"):
    m = a.shape[0]
    kdim = a.shape[1] if a_width is None else a_width
    n = ws[0].shape[-1] if n_out is None else n_out
    tm = _pick(m, (tm, 512, 256, 128, 64, 32, 16, 8))
    tn = _pick(n, (tn, 512, 256, 128))
    tk = kdim if tk is None else _pick(kdim, (tk,))
    nk = kdim // tk
    kb0 = a_col_block * nk
    if nk == 1:
        grid = (m // tm, n // tn)
        a_spec = pl.BlockSpec((tm, tk), lambda i, j: (i, kb0))
        w_spec_list = w_specs or [pl.BlockSpec((tk, tn), lambda i, j: (0, j)) for _ in ws]
        o_spec = pl.BlockSpec((tm, tn), lambda i, j: (i, j))
        sem = ("parallel", "parallel")
        scratch = []
    else:
        grid = (m // tm, n // tn, nk)
        a_spec = pl.BlockSpec((tm, tk), lambda i, j, k: (i, kb0 + k))
        w_spec_list = w_specs or [pl.BlockSpec((tk, tn), lambda i, j, k: (k, j)) for _ in ws]
        o_spec = pl.BlockSpec((tm, tn), lambda i, j, k: (i, j))
        sem = ("parallel", "parallel", "arbitrary")
        scratch = [pltpu.VMEM((tm, tn), F32) for _ in ws]
    extra_specs = [pl.BlockSpec(bs, im) for (_, bs, im) in extras]
    vm = 2 * _nbytes((tm, tk), a.dtype) + 2 * len(ws) * _nbytes((tk, tn), ws[0].dtype)
    vm += 2 * _nbytes((tm, tn), out_dtype) + (len(ws) + 2) * _nbytes((tm, tn), F32)
    for (arr, bs, _) in extras:
        vm += 2 * _nbytes([1 if b is None else b for b in bs], arr.dtype)
    return pl.pallas_call(
        functools.partial(_mm_kernel, nk=nk, n_w=len(ws), n_extra=len(extras), epi=epi),
        out_shape=jax.ShapeDtypeStruct((m, n), out_dtype),
        grid=grid,
        in_specs=[a_spec] + list(w_spec_list) + extra_specs,
        out_specs=o_spec,
        scratch_shapes=scratch,
        compiler_params=_params(sem, vm),
        name=name,
    )(a, *ws, *[e[0] for e in extras])


def _epi_id(accs, extras):
    return accs[0]


def _epi_residual(accs, extras):
    return extras[0] + accs[0]


def _epi_swiglu(accs, extras):
    g, u = accs
    return g * jax.nn.sigmoid(g) * u


def _epi_gelu(accs, extras):
    z = accs[0]
    return 0.5 * z * (1.0 + lax.erf(z * (2.0 ** -0.5)))


def _res_extra(h, tm, tn, nk):
    if nk == 1:
        return (h, (tm, tn), lambda i, j: (i, j))
    return (h, (tm, tn), lambda i, j, k: (i, j))


def _mix_out_kernel(a1_ref, a2_ref, w1_ref, w2_ref, h_ref, o_ref):
    o_ref[...] = (h_ref[...] + jnp.dot(a1_ref[...], w1_ref[...], preferred_element_type=F32)
                  + jnp.dot(a2_ref[...], w2_ref[...], preferred_element_type=F32))


def mix_out_proj(a1, a2, w, h):
    m, kh = a1.shape
    n = w.shape[1]
    assert a2.shape == a1.shape and w.shape[0] == 2 * kh
    tm = _pick(m, (1024, 512, 256, 128, 64, 32, 16, 8))
    tn = _pick(n, (512, 256, 128))
    vm = 2 * (2 * _nbytes((tm, kh), BF16) + 2 * _nbytes((kh, tn), BF16) + 2 * _nbytes((tm, tn), F32))
    vm += 3 * _nbytes((tm, tn), F32)
    return pl.pallas_call(
        _mix_out_kernel,
        out_shape=jax.ShapeDtypeStruct((m, n), F32),
        grid=(m // tm, n // tn),
        in_specs=[pl.BlockSpec((tm, kh), lambda i, j: (i, 0)),
                  pl.BlockSpec((tm, kh), lambda i, j: (i, 0)),
                  pl.BlockSpec((kh, tn), lambda i, j: (0, j)),
                  pl.BlockSpec((kh, tn), lambda i, j: (1, j)),
                  pl.BlockSpec((tm, tn), lambda i, j: (i, j))],
        out_specs=pl.BlockSpec((tm, tn), lambda i, j: (i, j)),
        compiler_params=_params(("parallel", "parallel"), vm),
        name="even_out_proj",
    )(a1, a2, w, w, h)


def _ple_kernel(hn_ref, wg_ref, p_ref, wp_ref, h_ref, o_ref):
    gate = jax.nn.sigmoid(jnp.dot(hn_ref[...], wg_ref[...], preferred_element_type=F32))
    proj = jnp.dot(p_ref[...], wp_ref[...], preferred_element_type=F32)
    o_ref[...] = h_ref[...] + gate * proj


def ple(hn, w_gate, p, w_proj, h):
    m, d = hn.shape
    n = w_gate.shape[1]
    pd = p.shape[1]
    tm = _pick(m, (1024, 512, 256, 128, 64, 32, 16, 8))
    tn = _pick(n, (512, 256, 128))
    vm = 2 * (_nbytes((tm, d), BF16) + _nbytes((d, tn), BF16) + _nbytes((tm, pd), BF16) + _nbytes((pd, tn), BF16))
    vm += 6 * _nbytes((tm, tn), F32)
    return pl.pallas_call(
        _ple_kernel,
        out_shape=jax.ShapeDtypeStruct((m, n), F32),
        grid=(m // tm, n // tn),
        in_specs=[pl.BlockSpec((tm, d), lambda i, j: (i, 0)),
                  pl.BlockSpec((d, tn), lambda i, j: (0, j)),
                  pl.BlockSpec((tm, pd), lambda i, j: (i, 0)),
                  pl.BlockSpec((pd, tn), lambda i, j: (0, j)),
                  pl.BlockSpec((tm, tn), lambda i, j: (i, j))],
        out_specs=pl.BlockSpec((tm, tn), lambda i, j: (i, j)),
        compiler_params=_params(("parallel", "parallel"), vm),
        name="ple",
    )(hn, w_gate, p, w_proj, h)


def _bdot(a, b):
    return jnp.dot(a.astype(BF16), b.astype(BF16), preferred_element_type=F32)


def _bdot_nt(a, b):
    return lax.dot_general(a.astype(BF16), b.astype(BF16), (((1,), (1,)), ((), ())),
                           preferred_element_type=F32)


def _bdot_tn(a, b):
    return lax.dot_general(a.astype(BF16), b.astype(BF16), (((0,), (0,)), ((), ())),
                           preferred_element_type=F32)


def _split3(x):
    hi = x.astype(BF16)
    r1 = x - hi.astype(F32)
    mid = r1.astype(BF16)
    lo = (r1 - mid.astype(F32)).astype(BF16)
    return hi, mid, lo


def _bmm(a, b):
    return jnp.einsum("bij,bjk->bik", a.astype(BF16), b.astype(BF16), preferred_element_type=F32)


def _bmm_nt(a, b):
    return jnp.einsum("bik,bjk->bij", a.astype(BF16), b.astype(BF16), preferred_element_type=F32)


def _bmm_tn(a, b):
    return jnp.einsum("bti,btj->bij", a.astype(BF16), b.astype(BF16), preferred_element_type=F32)


def _rwkv_a_kernel(r_ref, k_ref, v_ref, ai_ref, lw_ref, kk_ref, ka_ref, mt_ref, nt_ref, q_ref, o0_ref, *, hb, ncb):
    t = RW_CHUNK
    n = RW_HEAD
    nb = hb * ncb
    row = lax.broadcasted_iota(jnp.int32, (t, t), 0)
    col = lax.broadcasted_iota(jnp.int32, (t, t), 1)
    incl = col <= row
    strict = col < row
    tri = jnp.broadcast_to(incl.astype(BF16), (nb, t, t))
    eye = (row == col).astype(F32)
    drow = lax.broadcasted_iota(jnp.int32, (n, n), 0)
    dcol = lax.broadcasted_iota(jnp.int32, (n, n), 1)

    def units(ref):
        return jnp.stack([ref[c * t:(c + 1) * t, h * n:(h + 1) * n] for c in range(ncb) for h in range(hb)])

    def unit_params(ref):
        return jnp.stack([ref[:, h * n:(h + 1) * n] for c in range(ncb) for h in range(hb)])

    k = units(k_ref)
    ai = units(ai_ref)
    kk = k * unit_params(kk_ref)
    kk = kk * lax.rsqrt(jnp.maximum(jnp.sum(kk * kk, axis=-1, keepdims=True), 1e-24))
    k = k * (1.0 + (ai - 1.0) * unit_params(ka_ref))
    a = -kk
    b = kk * ai
    lw = units(lw_ref)
    hi, mid, lo = _split3(lw)
    cum = (jnp.einsum("bij,bjk->bik", tri, hi, preferred_element_type=F32)
           + jnp.einsum("bij,bjk->bik", tri, mid, preferred_element_type=F32)
           + jnp.einsum("bij,bjk->bik", tri, lo, preferred_element_type=F32))
    cum_t = cum[:, t - 1:t, :]
    e_in = jnp.exp(cum)
    e_out = jnp.exp(-cum)
    e_prev = jnp.exp(cum - lw)
    e_rem = jnp.exp(cum_t - cum)
    p_t = jnp.exp(cum_t)
    r = units(r_ref)
    v = units(v_ref)
    rh = r * e_in
    ah = a * e_prev
    bh = b * e_out
    kh = k * e_out
    bt = b * e_rem
    kt = k * e_rem
    g = _bmm_nt(jnp.concatenate([ah, rh], axis=1), jnp.concatenate([bh, kh], axis=1))
    row2 = lax.broadcasted_iota(jnp.int32, (t, 2 * t), 0)
    col2 = lax.broadcasted_iota(jnp.int32, (t, 2 * t), 1) % t
    n_cat = jnp.where(col2 < row2, g[:, :t, :], 0.0)
    l_cat = jnp.where(col2 <= row2, g[:, t:, :], 0.0)
    n_ab = n_cat[:, :, :t]
    x = eye + n_ab
    pw = n_ab
    for _ in range((t - 1).bit_length() - 1):
        pw = _bmm(pw, pw)
        x = x + _bmm(x, pw)
    zeros = jnp.zeros_like(v)
    akv = _bmm(n_cat, jnp.concatenate([zeros, v], axis=1))
    wu = _bmm(x, jnp.concatenate([ah, akv], axis=2))
    wu_v = jnp.concatenate([wu, jnp.concatenate([zeros, v], axis=2)], axis=1)
    l2 = _bmm(l_cat, wu_v)
    b2 = _bmm_tn(jnp.concatenate([bt, kt], axis=1), wu_v)
    qt = rh + l2[:, :, :n]
    o0 = l2[:, :, n:]
    mt = jnp.where(drow == dcol, p_t, 0.0) + b2[:, :, :n]
    nt = b2[:, :, n:]
    for c in range(ncb):
        rows = slice(c * t, (c + 1) * t)
        q_ref[rows, :] = jnp.concatenate([qt[c * hb + h] for h in range(hb)], axis=-1)
        o0_ref[rows, :] = jnp.concatenate([o0[c * hb + h] for h in range(hb)], axis=-1)
        mt_ref[:, c] = mt[c * hb:(c + 1) * hb]
        nt_ref[:, c] = nt[c * hb:(c + 1) * hb]


def _bdot3(a, b):
    a_hi = a.astype(BF16)
    a_lo = (a - a_hi.astype(F32)).astype(BF16)
    b_hi = b.astype(BF16)
    b_lo = (b - b_hi.astype(F32)).astype(BF16)
    mm = lambda x, y: jnp.einsum("bij,bjk->bik", x, y, preferred_element_type=F32)
    return mm(a_hi, b_hi) + mm(a_hi, b_lo) + mm(a_lo, b_hi)


def _rwkv_b_kernel(mt_ref, nt_ref, q_ref, o0_ref, r_ref, k_ref, v_ref, ai_ref, g_ref, ka_ref, rk_ref, lw_ref, lb_ref,
                   o_ref, z_ref, *, nh):
    n = RW_HEAD
    c = pl.program_id(0)

    @pl.when(c == 0)
    def _():
        z_ref[...] = jnp.zeros_like(z_ref)

    def heads(ref):
        return jnp.stack([ref[:, h * n:(h + 1) * n] for h in range(nh)])

    z = z_ref[...]
    o = heads(o0_ref) + _bdot3(heads(q_ref), z)
    z_ref[...] = nt_ref[:, 0] + _bdot3(mt_ref[:, 0], z)
    mean = jnp.mean(o, axis=-1, keepdims=True)
    var = jnp.mean(jnp.square(o - mean), axis=-1, keepdims=True)
    o = (o - mean) * lax.rsqrt(var + GN_EPS) * heads(lw_ref) + heads(lb_ref)
    k = heads(k_ref) * (1.0 + (heads(ai_ref) - 1.0) * heads(ka_ref))
    bonus = jnp.sum(heads(r_ref) * k * heads(rk_ref), axis=-1, keepdims=True) * heads(v_ref)
    out = (o + bonus) * heads(g_ref)
    o_ref[...] = jnp.concatenate([out[h] for h in range(nh)], axis=-1).astype(o_ref.dtype)


def rwkv_mix(r, k, v, ai, lw, g, k_k, k_a, r_k, lnx_w, lnx_b):
    s, dim = r.shape
    n = RW_HEAD
    nh = dim // n
    t = RW_CHUNK
    nc = s // t
    hb = _pick(nh, (RW_HEADS_PER_STEP, 4, 2))
    ncb = _pick(nc, (RW_CHUNKS_PER_STEP, 2, 1))
    row = lambda p: p.reshape(1, dim).astype(F32)
    seq_spec = pl.BlockSpec((ncb * t, hb * n), lambda g_, c: (c, g_))
    par_spec = pl.BlockSpec((1, hb * n), lambda g_, c: (0, g_))
    mat_spec = pl.BlockSpec((hb, ncb, n, n), lambda g_, c: (g_, c, 0, 0))
    mt, nt, qt, o0 = pl.pallas_call(
        functools.partial(_rwkv_a_kernel, hb=hb, ncb=ncb),
        out_shape=(jax.ShapeDtypeStruct((nh, nc, n, n), F32), jax.ShapeDtypeStruct((nh, nc, n, n), F32),
                   jax.ShapeDtypeStruct((s, dim), F32), jax.ShapeDtypeStruct((s, dim), F32)),
        grid=(nh // hb, nc // ncb),
        in_specs=[seq_spec] * 5 + [par_spec] * 2,
        out_specs=(mat_spec, mat_spec, seq_spec, seq_spec),
        compiler_params=_params(("parallel", "parallel"), 32 << 20),
        name="rwkv_chunk_local",
    )(r, k, v, ai, lw, row(k_k), row(k_a))
    seq_all = pl.BlockSpec((t, dim), lambda c: (c, 0))
    par_all = pl.BlockSpec((1, dim), lambda c: (0, 0))
    mat_all = pl.BlockSpec((nh, 1, n, n), lambda c: (0, c, 0, 0))
    return pl.pallas_call(
        functools.partial(_rwkv_b_kernel, nh=nh),
        out_shape=jax.ShapeDtypeStruct((s, dim), BF16),
        grid=(nc,),
        in_specs=[mat_all, mat_all] + [seq_all] * 7 + [par_all] * 4,
        out_specs=seq_all,
        scratch_shapes=[pltpu.VMEM((nh, n, n), F32)],
        compiler_params=_params(("arbitrary",), 32 << 20),
        name="rwkv_state_scan",
    )(mt, nt, qt, o0, r, k, v, ai, g, row(k_a), row(r_k), row(lnx_w), row(lnx_b))


def _flash_kernel(qn_ref, qp_ref, kn_ref, kp_ref, v_ref, o_ref, q_sc, m_sc, acc_sc, s0_sc, s1_sc, *, t):
    i = pl.program_id(1)
    nt = t // LANES
    q_sc[...] = jnp.concatenate([qn_ref[...], qp_ref[...]], axis=1)
    m_sc[...] = jnp.full_like(m_sc, -1e30)
    acc_sc[...] = jnp.zeros_like(acc_sc)
    ones = jnp.ones((t, LANES), BF16)

    def scores(j, s_ref):
        start = pl.multiple_of(j * t, t)
        kb = jnp.concatenate([kn_ref[pl.ds(start, t), :], kp_ref[pl.ds(start, t), :]], axis=1)
        s_ref[...] = lax.dot_general(q_sc[...], kb, (((1,), (1,)), ((), ())), preferred_element_type=F32)

    def softmax_pv(j, s_ref, masked):
        start = pl.multiple_of(j * t, t)
        vb = jnp.concatenate([v_ref[pl.ds(start, t), :], ones], axis=1)
        tiles = [s_ref[:, c * LANES:(c + 1) * LANES] for c in range(nt)]
        if masked:
            rows = lax.broadcasted_iota(jnp.int32, (t, LANES), 0)
            cols = lax.broadcasted_iota(jnp.int32, (t, LANES), 1)
            tiles = [jnp.where(cols + c * LANES <= rows, tl, -1e30) for c, tl in enumerate(tiles)]
        mx = tiles[0]
        for c in range(1, nt):
            mx = jnp.maximum(mx, tiles[c])
        m_prev = m_sc[...]
        m_new = jnp.maximum(m_prev, jnp.max(mx, axis=1, keepdims=True))
        alpha = jnp.exp2(m_prev - m_new)
        p = jnp.concatenate([jnp.exp2(tl - m_new).astype(BF16) for tl in tiles], axis=1)
        pv = jnp.dot(p, vb, preferred_element_type=F32)
        acc_sc[...] = jnp.concatenate([alpha, alpha], axis=1) * acc_sc[...] + pv
        m_sc[...] = m_new

    def stage(j, s_cur, s_next):
        scores(j + 1, s_next)
        softmax_pv(j, s_cur, False)

    scores(0, s0_sc)

    def pair(jp, carry):
        stage(2 * jp, s0_sc, s1_sc)
        stage(2 * jp + 1, s1_sc, s0_sc)
        return carry

    lax.fori_loop(0, i // 2, pair, 0)

    @pl.when(i % 2 == 1)
    def _():
        stage(i - 1, s0_sc, s1_sc)
        softmax_pv(i, s1_sc, True)

    @pl.when(i % 2 == 0)
    def _():
        softmax_pv(i, s0_sc, True)

    o_ref[...] = (acc_sc[:, :LANES] / acc_sc[:, LANES:]).astype(o_ref.dtype)


def flash_attention(q_nope, q_pe, kv, k_pe, nh):
    s = q_nope.shape[0]
    dh = LANES
    tq = _pick(s, (FLASH_T, 512, 256, 128))
    vm = 2 * (3 * _nbytes((s, dh), BF16) + 3 * _nbytes((tq, dh), BF16))
    vm += 5 * _nbytes((tq, tq), F32) + 4 * _nbytes((tq, 2 * dh), F32)
    scratch = [pltpu.VMEM((tq, 2 * dh), BF16), pltpu.VMEM((tq, dh), F32), pltpu.VMEM((tq, 2 * dh), F32),
               pltpu.VMEM((tq, tq), F32), pltpu.VMEM((tq, tq), F32)]
    return pl.pallas_call(
        functools.partial(_flash_kernel, t=tq),
        out_shape=jax.ShapeDtypeStruct((s, nh * dh), BF16),
        grid=(nh, s // tq),
        in_specs=[pl.BlockSpec((tq, dh), lambda h, i: (i, h)),
                  pl.BlockSpec((tq, dh), lambda h, i: (i, h)),
                  pl.BlockSpec((s, dh), lambda h, i: (0, h)),
                  pl.BlockSpec((s, dh), lambda h, i: (0, 0)),
                  pl.BlockSpec((s, dh), lambda h, i: (0, nh + h))],
        out_specs=pl.BlockSpec((tq, dh), lambda h, i: (i, h)),
        scratch_shapes=scratch,
        compiler_params=_params(("parallel", "arbitrary"), vm),
        name="mla_flash_attention",
    )(q_nope, q_pe, kv, k_pe, kv)


def _spatial_kernel(w_ref, bias_ref, v_ref, u_ref, o_ref, *, nch):
    c = CHUNK
    row = lax.broadcasted_iota(jnp.int32, (c, c), 0)
    col = lax.broadcasted_iota(jnp.int32, (c, c), 1)
    w = jnp.where(col <= row, w_ref[0], 0.0).astype(BF16)
    bias = bias_ref[0]
    for ci in range(nch):
        sl = pl.ds(ci * c, c)
        sg = jnp.dot(w, v_ref[sl, :], preferred_element_type=F32) + bias
        o_ref[sl, :] = (u_ref[sl, :].astype(F32) * sg).astype(o_ref.dtype)


def spatial_gate(z, vn, w_sp, b_sp):
    s, d = vn.shape
    g = w_sp.shape[0]
    gd = d // g
    nch = _pick(s // CHUNK, (16, 8, 4, 2, 1))
    rows = nch * CHUNK
    bias = jnp.broadcast_to(b_sp.astype(F32)[:, :, None], (g, CHUNK, gd))
    return pl.pallas_call(
        functools.partial(_spatial_kernel, nch=nch),
        out_shape=jax.ShapeDtypeStruct((s, d), BF16),
        grid=(g, s // rows),
        in_specs=[pl.BlockSpec((1, CHUNK, CHUNK), lambda gi, ci: (gi, 0, 0)),
                  pl.BlockSpec((1, CHUNK, gd), lambda gi, ci: (gi, 0, 0)),
                  pl.BlockSpec((rows, gd), lambda gi, ci: (ci, gi)),
                  pl.BlockSpec((rows, gd), lambda gi, ci: (ci, gi))],
        out_specs=pl.BlockSpec((rows, gd), lambda gi, ci: (ci, gi)),
        compiler_params=_params(("parallel", "parallel"), 16 << 20),
        name="spatial_gate",
    )(w_sp.astype(F32), bias, vn, z)


def _norm_router_kernel(h_ref, g_ref, wr_ref, xn_ref, lg_ref, *, eps):
    x = h_ref[...]
    ms = jnp.mean(x * x, axis=-1, keepdims=True)
    xn = x * lax.rsqrt(ms + eps) * g_ref[...]
    x_hi = xn.astype(BF16)
    half = xn.shape[1] // 2
    rounded = x_hi.astype(F32)
    lo_bits = pltpu.bitcast(rounded[:, :half], jnp.uint32) >> 16
    hi_bits = pltpu.bitcast(rounded[:, half:], jnp.uint32) & jnp.uint32(0xFFFF0000)
    xn_ref[...] = lo_bits | hi_bits
    x_lo = (xn - x_hi.astype(F32)).astype(BF16)
    w = wr_ref[...]
    w_hi = w.astype(BF16)
    w_lo = (w - w_hi.astype(F32)).astype(BF16)
    lg_ref[...] = (jnp.dot(x_hi, w_hi, preferred_element_type=F32) + jnp.dot(x_hi, w_lo, preferred_element_type=F32)
                   + jnp.dot(x_lo, w_hi, preferred_element_type=F32))


def norm_router(h, g, w_router):
    m, d = h.shape
    e = w_router.shape[1]
    wr = jnp.pad(w_router.astype(F32), ((0, 0), (0, LANES - e)))
    tm = _pick(m, (512, 256, 128, 64, 32, 16, 8))
    vm = 2 * (2 * _nbytes((tm, d), F32) + _nbytes((d, LANES), F32)) + 6 * _nbytes((tm, d), F32)
    xn, lg = pl.pallas_call(
        functools.partial(_norm_router_kernel, eps=RMS_EPS),
        out_shape=(jax.ShapeDtypeStruct((m, d // 2), jnp.uint32), jax.ShapeDtypeStruct((m, LANES), F32)),
        grid=(m // tm,),
        in_specs=[pl.BlockSpec((tm, d), lambda i: (i, 0)),
                  pl.BlockSpec((1, d), lambda i: (0, 0)),
                  pl.BlockSpec((d, LANES), lambda i: (0, 0))],
        out_specs=(pl.BlockSpec((tm, d // 2), lambda i: (i, 0)), pl.BlockSpec((tm, LANES), lambda i: (i, 0))),
        compiler_params=_params(("parallel",), vm),
        name="norm_router",
    )(h, g.reshape(1, d).astype(F32), wr)
    return xn, lg[:, :e]


def _row_copy(src_hbm, src_row, dst, dst_row, sem):
    return pltpu.make_async_copy(src_hbm.at[pl.ds(src_row, 1)], dst.at[pl.ds(dst_row, 1)], sem)


def _moe_up_kernel(te_ref, tv_ref, src_ref, x_hbm, wg_ref, wu_ref, gate_ref, o_ref, xbuf, xb16, sems, *, tile):
    t = pl.program_id(0)
    n_tiles = pl.num_programs(0)
    slot = t % 2

    def start_tile(tt, sl):
        def body(r, carry):
            _row_copy(x_hbm, src_ref[tt * tile + r], xbuf.at[sl], r, sems.at[sl]).start()
            return carry
        lax.fori_loop(0, tile, body, 0)

    def wait_tile(sl):
        def body(r, carry):
            _row_copy(x_hbm, 0, xbuf.at[sl], r, sems.at[sl]).wait()
            return carry
        lax.fori_loop(0, tile, body, 0)

    @pl.when(pl.program_id(1) == 0)
    def _():
        @pl.when(t == 0)
        def _():
            start_tile(0, 0)

        wait_tile(slot)

        @pl.when(t + 1 < n_tiles)
        def _():
            start_tile(t + 1, 1 - slot)

        words = xbuf[slot]
        lo = pltpu.bitcast(words << 16, F32)
        hi = pltpu.bitcast(words & jnp.uint32(0xFFFF0000), F32)
        xb16[...] = jnp.concatenate([lo, hi], axis=1).astype(BF16)

    @pl.when(tv_ref[t] == 1)
    def _():
        x = xb16[...]
        g = jnp.dot(x, wg_ref[...], preferred_element_type=F32)
        u = jnp.dot(x, wu_ref[...], preferred_element_type=F32)
        o_ref[...] = (gate_ref[...] * (g * jax.nn.sigmoid(g) * u)).astype(o_ref.dtype)

    @pl.when(tv_ref[t] == 0)
    def _():
        o_ref[...] = jnp.zeros_like(o_ref)


def _moe_down_kernel(te_ref, tv_ref, a_ref, w_ref, o_ref):
    t = pl.program_id(0)

    @pl.when(tv_ref[t] == 1)
    def _():
        o_ref[...] = jnp.dot(a_ref[...], w_ref[...], preferred_element_type=F32)

    @pl.when(tv_ref[t] == 0)
    def _():
        o_ref[...] = jnp.zeros_like(o_ref)


def _moe_combine_kernel(s1_ref, s2_ref, h_ref, g_ref, d_hbm, o_ref, on_ref, buf, sem, *, tm):
    base = pl.program_id(0) * tm

    def start(r, carry):
        _row_copy(d_hbm, s1_ref[base + r], buf.at[0], r, sem).start()
        _row_copy(d_hbm, s2_ref[base + r], buf.at[1], r, sem).start()
        return carry

    lax.fori_loop(0, tm, start, 0)

    def wait(r, carry):
        _row_copy(d_hbm, 0, buf.at[0], r, sem).wait()
        _row_copy(d_hbm, 0, buf.at[1], r, sem).wait()
        return carry

    lax.fori_loop(0, tm, wait, 0)
    o = h_ref[...] + buf[0] + buf[1]
    o_ref[...] = o
    ms = jnp.mean(o * o, axis=-1, keepdims=True)
    on_ref[...] = (o * lax.rsqrt(ms + RMS_EPS) * g_ref[...]).astype(on_ref.dtype)


def _route(logits, tile):
    s, ne = logits.shape
    top_val, top_idx = lax.top_k(logits, TOP_K)
    top_w = jax.nn.softmax(top_val, axis=-1)
    e_flat = top_idx.reshape(-1).astype(jnp.int32)
    onehot = (e_flat[:, None] == jnp.arange(ne, dtype=jnp.int32)[None, :]).astype(jnp.int32)
    csum = jnp.cumsum(onehot, axis=0)
    rank = jnp.sum(csum * onehot, axis=1) - 1
    counts = csum[-1]
    padded = (counts + tile - 1) // tile * tile
    ends = jnp.cumsum(padded)
    slot = (ends - padded)[e_flat] + rank
    p_rows = s * TOP_K + ne * tile
    tok = (jnp.arange(s * TOP_K, dtype=jnp.int32) // TOP_K).astype(F32)
    placed = jnp.zeros((p_rows, 2), F32).at[slot].set(jnp.stack([tok, top_w.reshape(-1)], axis=1))
    src = placed[:, 0].astype(jnp.int32)
    gate = placed[:, 1]
    tstart = jnp.arange(p_rows // tile, dtype=jnp.int32) * tile
    tile_e = jnp.minimum(jnp.sum((tstart[:, None] >= ends[None, :]).astype(jnp.int32), axis=1), ne - 1)
    tile_valid = (tstart < ends[-1]).astype(jnp.int32)
    return src, gate.reshape(p_rows, 1), tile_e, tile_valid, slot.reshape(s, TOP_K)


def _rw_prep_kernel(yr_ref, yk_ref, yv_ref, yl_ref, pr_ref, pk_ref, pv_ref, pl_ref,
                    mur_ref, muk_ref, muv_ref, mul_ref, w2_ref, a2_ref, g2_ref, w0_ref, a0_ref,
                    r_ref, k_ref, v_ref, lw_ref, ai_ref, g_ref):
    first = pl.program_id(0) == 0

    def shifted(x_ref, p_ref, mu_ref):
        x = x_ref[...].astype(F32)
        prev = jnp.where(first, 0.0, p_ref[...].astype(F32)[7:8, :])
        row = lax.broadcasted_iota(jnp.int32, x.shape, 0)
        xp = jnp.where(row == 0, prev, pltpu.roll(x, 1, axis=0))
        return x + (xp - x) * mu_ref[...]

    r_ref[...] = shifted(yr_ref, pr_ref, mur_ref)
    k_ref[...] = shifted(yk_ref, pk_ref, muk_ref)
    v_ref[...] = shifted(yv_ref, pv_ref, muv_ref)
    lo = shifted(yl_ref, pl_ref, mul_ref)
    lane = lax.broadcasted_iota(jnp.int32, lo.shape, 1)
    act = jnp.where(lane < DECAY_LORA, jnp.tanh(lo),
                    jnp.where(lane < DECAY_LORA + AAA_LORA, lo,
                              jnp.where(lane < DECAY_LORA + AAA_LORA + GATE_LORA, jax.nn.sigmoid(lo), 0.0)))
    act = act.astype(BF16)
    w_pre = w0_ref[...] + jnp.dot(act, w2_ref[...], preferred_element_type=F32)
    x = -w_pre
    softplus = jnp.maximum(x, 0.0) + jnp.log(1.0 + jnp.exp(-jnp.abs(x)))
    lw_ref[...] = -jnp.exp(-softplus - 0.5)
    ai_ref[...] = jax.nn.sigmoid(a0_ref[...] + jnp.dot(act, a2_ref[...], preferred_element_type=F32))
    g_ref[...] = jnp.dot(act, g2_ref[...], preferred_element_type=F32)


def _rwkv_group(y, col_lo, mu, w0, w2, a0, a2, g2, k_k, k_a, r_k, lnx_w, lnx_b):
    s = y.shape[0]
    dim = RW_DIM
    lora = DECAY_LORA + AAA_LORA + GATE_LORA
    lwid = -(-lora // LANES) * LANES
    assert col_lo % lwid == 0 and dim % LANES == 0
    tm = _pick(s, (128, 64, 32, 16, 8))
    pb = tm // 8
    mu_rkv = mu[:3 * dim].reshape(1, 3 * dim)
    mu_lo = jnp.pad(mu[3 * dim:], (0, lwid - lora)).reshape(1, lwid)

    def pad_rows(w, start):
        return jnp.pad(w, ((start, lwid - start - w.shape[0]), (0, 0))).astype(BF16)

    y_spec = lambda c: pl.BlockSpec((tm, dim), lambda i: (i, c))
    p_spec = lambda c: pl.BlockSpec((8, dim), lambda i: (jnp.maximum(i * pb - 1, 0), c))
    mu_spec = lambda c: pl.BlockSpec((1, dim), lambda i: (0, c))
    cl = col_lo // lwid
    w_spec = pl.BlockSpec((lwid, dim), lambda i: (0, 0))
    row_spec = pl.BlockSpec((1, dim), lambda i: (0, 0))
    out_spec = pl.BlockSpec((tm, dim), lambda i: (i, 0))
    out_sds = jax.ShapeDtypeStruct((s, dim), F32)
    vm = 2 * (3 * _nbytes((tm, dim), BF16) + 6 * _nbytes((tm, dim), F32) + 3 * _nbytes((lwid, dim), BF16))
    vm += 6 * _nbytes((tm, dim), F32)
    r, k, v, lw, ai, g = pl.pallas_call(
        _rw_prep_kernel,
        out_shape=(out_sds,) * 6,
        grid=(s // tm,),
        in_specs=[y_spec(0), y_spec(1), y_spec(2), pl.BlockSpec((tm, lwid), lambda i: (i, cl)),
                  p_spec(0), p_spec(1), p_spec(2),
                  pl.BlockSpec((8, lwid), lambda i: (jnp.maximum(i * pb - 1, 0), cl)),
                  mu_spec(0), mu_spec(1), mu_spec(2), pl.BlockSpec((1, lwid), lambda i: (0, 0)),
                  w_spec, w_spec, w_spec, row_spec, row_spec],
        out_specs=(out_spec,) * 6,
        compiler_params=_params(("parallel",), vm),
        name="rwkv_prep",
    )(y, y, y, y, y, y, y, y, mu_rkv, mu_rkv, mu_rkv, mu_lo,
      pad_rows(w2, 0), pad_rows(a2, DECAY_LORA), pad_rows(g2, DECAY_LORA + AAA_LORA),
      w0.reshape(1, dim), a0.reshape(1, dim))
    return rwkv_mix(r, k, v, ai, lw, g, k_k, k_a, r_k.reshape(-1), lnx_w, lnx_b)


def _rope(x, cos, sin):
    half = x.shape[-1] // 2
    x1, x2 = x[..., :half], x[..., half:]
    return jnp.concatenate([x1 * cos - x2 * sin, x2 * cos + x1 * sin], axis=-1)


def _mla_group(y, col_q, col_kv, col_pe, cos, sin, q_norm, w_uq, kv_norm, w_ukv):
    s = y.shape[0]
    nh = MLA_HEADS
    cqn = rmsnorm(y, q_norm, BF16, width=Q_LORA, col_block=col_q // Q_LORA)
    ckvn = rmsnorm(y, kv_norm, BF16, width=KV_LORA, col_block=col_kv // KV_LORA)
    assert QK_NOPE == LANES and V_HEAD == LANES and QK_ROPE <= LANES
    half = QK_ROPE // 2
    scale = (QK_NOPE + QK_ROPE) ** -0.5 * 1.4426950408889634
    wq = w_uq.reshape(Q_LORA, nh, QK_NOPE + QK_ROPE)
    w_nope = wq[:, :, :QK_NOPE].reshape(Q_LORA, nh * LANES).astype(BF16)
    w_pe = wq[:, :, QK_NOPE:]
    w_rot = jnp.concatenate([-w_pe[..., half:], w_pe[..., :half]], axis=-1)
    lane_pad = ((0, 0), (0, 0), (0, LANES - QK_ROPE))
    w_pe = jnp.pad(w_pe, lane_pad).reshape(Q_LORA, nh * LANES).astype(BF16)
    w_rot = jnp.pad(w_rot, lane_pad).reshape(Q_LORA, nh * LANES).astype(BF16)
    tm = _pick(s, (1024, 512, 256, 128, 64, 32, 16, 8))
    tn = _pick(nh * LANES, (512, 256, 128))
    table_pad = ((0, 0), (0, LANES - QK_ROPE))
    cos_t = jnp.tile(jnp.pad(jnp.concatenate([cos, cos], axis=-1), table_pad), (1, tn // LANES))
    sin_t = jnp.tile(jnp.pad(jnp.concatenate([sin, sin], axis=-1), table_pad), (1, tn // LANES))
    q_nope = matmul(cqn, [w_nope], lambda accs, extras: accs[0] * scale, BF16, tm=tm, tn=tn, name="mla_q_nope_up")
    q_pe = matmul(cqn, [w_pe, w_rot], lambda accs, extras: (accs[0] * extras[0] + accs[1] * extras[1]) * scale,
                  BF16, tm=tm, tn=tn,
                  extras=[(cos_t, (tm, tn), lambda i, jn: (i, 0)), (sin_t, (tm, tn), lambda i, jn: (i, 0))],
                  name="mla_q_rope_up")
    w_kv = jnp.transpose(w_ukv.reshape(KV_LORA, nh, 2, LANES), (0, 2, 1, 3)).reshape(KV_LORA, 2 * nh * LANES)
    kv = matmul(ckvn, [w_kv.astype(BF16)], _epi_id, BF16, tm=tm, tn=tn, name="mla_kv_up")
    k_pe = _rope(y[:, col_pe:col_pe + QK_ROPE].astype(F32), cos, sin)
    k_pe = jnp.pad(k_pe, table_pad).astype(BF16)
    return flash_attention(q_nope, q_pe, kv, k_pe, nh)


def _even_layer(h, hn, j, cos, sin, prm):
    w_in = prm["ev_in_proj"][j]
    rw_end = 3 * RW_DIM
    lora_w = DECAY_LORA + AAA_LORA + GATE_LORA
    rw_proj = rw_end + lora_w
    w_perm = jnp.concatenate([
        w_in[:, :rw_end],
        w_in[:, rw_proj:rw_proj + Q_LORA + KV_LORA],
        w_in[:, rw_end:rw_proj],
        w_in[:, rw_proj + Q_LORA + KV_LORA:],
    ], axis=1).astype(BF16)
    y = matmul(hn, [w_perm], _epi_id, BF16, tm=1024, tn=512, name="even_in_proj")
    col_q = rw_end
    col_kv = rw_end + Q_LORA
    col_lo = rw_end + Q_LORA + KV_LORA
    col_pe = col_lo + lora_w
    mu = prm["rw_mu"][j]
    o_rw = _rwkv_group(y, col_lo, mu, prm["rw_w0"][j], prm["rw_w2"][j], prm["rw_a0"][j], prm["rw_a2"][j],
                       prm["rw_g2"][j], prm["rw_k_k"][j], prm["rw_k_a"][j], prm["rw_r_k"][j],
                       prm["rw_lnx_w"][j], prm["rw_lnx_b"][j])
    o_mla = _mla_group(y, col_q, col_kv, col_pe, cos, sin, prm["mla_q_norm"][j], prm["mla_w_uq"][j],
                       prm["mla_kv_norm"][j], prm["mla_w_ukv"][j])
    return mix_out_proj(o_rw, o_mla, prm["ev_out_proj"][j].astype(BF16), h)


def _dense_ffn(h, hn, wg, wu, wd):
    d, f = wg.shape
    fp = -(-f // 1024) * 1024 if f > 1024 else f
    pad = fp - f
    wg = jnp.pad(wg.astype(BF16), ((0, 0), (0, pad)))
    wu = jnp.pad(wu.astype(BF16), ((0, 0), (0, pad)))
    wd = jnp.pad(wd.astype(BF16), ((0, pad), (0, 0)))
    mid = matmul(hn, [wg, wu], _epi_swiglu, BF16, tm=1024, tn=512, name="ffn_up")
    tm = _pick(h.shape[0], (1024, 512, 256, 128, 64, 32, 16, 8))
    tn = _pick(h.shape[1], (1024, 512, 256, 128))
    tk = _pick(fp, (2816, 2048, 1024, 512, 256, 128))
    nk = fp // tk
    return matmul(mid, [wd], _epi_residual, F32, tm=tm, tn=tn, tk=tk,
                  extras=[_res_extra(h, tm, tn, nk)], name="ffn_down")


def _moe(h, g_norm, w_router, wg, wu, wd, g_next):
    ne, d, f = wg.shape
    s = h.shape[0]
    xn, logits = norm_router(h, g_norm, w_router)
    tile = _pick(s, (MOE_TILE, 256, 128, 64, 32, 16, 8))
    src, gate, tile_e, tile_valid, slots = _route(logits, tile)
    p_rows = src.shape[0]
    n_tiles = p_rows // tile
    tn = _pick(f, (512, 256, 128))
    vm = 2 * (2 * _nbytes((d, tn), BF16) + _nbytes((tile, tn), BF16))
    vm += 2 * _nbytes((tile, d), F32) + 2 * _nbytes((tile, d), BF16) + 4 * _nbytes((tile, tn), F32)
    w_spec = pl.BlockSpec((None, d, tn), lambda t, jn, te, tv, sr: (te[t], 0, jn))
    mid = pl.pallas_call(
        functools.partial(_moe_up_kernel, tile=tile),
        out_shape=jax.ShapeDtypeStruct((p_rows, f), BF16),
        grid_spec=pltpu.PrefetchScalarGridSpec(
            num_scalar_prefetch=3, grid=(n_tiles, f // tn),
            in_specs=[pl.BlockSpec(memory_space=pl.ANY), w_spec, w_spec,
                      pl.BlockSpec((tile, 1), lambda t, jn, te, tv, sr: (t, 0))],
            out_specs=pl.BlockSpec((tile, tn), lambda t, jn, te, tv, sr: (t, jn)),
            scratch_shapes=[pltpu.VMEM((2, tile, d // 2), jnp.uint32), pltpu.VMEM((tile, d), BF16),
                            pltpu.SemaphoreType.DMA((2,))]),
        compiler_params=_params(("arbitrary", "arbitrary"), vm),
        name="moe_up",
    )(tile_e, tile_valid, src, xn, wg.astype(BF16), wu.astype(BF16), gate)
    tn2 = _pick(d, (1024, 512, 256, 128))
    vm = 2 * (_nbytes((tile, f), BF16) + _nbytes((f, tn2), BF16) + _nbytes((tile, tn2), F32)) + _nbytes((tile, tn2), F32)
    dn = pl.pallas_call(
        _moe_down_kernel,
        out_shape=jax.ShapeDtypeStruct((p_rows, d), F32),
        grid_spec=pltpu.PrefetchScalarGridSpec(
            num_scalar_prefetch=2, grid=(n_tiles, d // tn2),
            in_specs=[pl.BlockSpec((tile, f), lambda t, jn, te, tv: (t, 0)),
                      pl.BlockSpec((None, f, tn2), lambda t, jn, te, tv: (te[t], 0, jn))],
            out_specs=pl.BlockSpec((tile, tn2), lambda t, jn, te, tv: (t, jn))),
        compiler_params=_params(("parallel", "parallel"), vm),
        name="moe_down",
    )(tile_e, tile_valid, mid, wd.astype(BF16))
    tm = _pick(s, (256, 128, 64, 32, 16, 8))
    vm = 4 * _nbytes((tm, d), F32) + 2 * _nbytes((tm, d), F32) + 2 * _nbytes((tm, d), F32)
    return pl.pallas_call(
        functools.partial(_moe_combine_kernel, tm=tm),
        out_shape=(jax.ShapeDtypeStruct((s, d), F32), jax.ShapeDtypeStruct((s, d), BF16)),
        grid_spec=pltpu.PrefetchScalarGridSpec(
            num_scalar_prefetch=2, grid=(s // tm,),
            in_specs=[pl.BlockSpec((tm, d), lambda i, s1, s2: (i, 0)), pl.BlockSpec((1, d), lambda i, s1, s2: (0, 0)),
                      pl.BlockSpec(memory_space=pl.ANY)],
            out_specs=(pl.BlockSpec((tm, d), lambda i, s1, s2: (i, 0)), pl.BlockSpec((tm, d), lambda i, s1, s2: (i, 0))),
            scratch_shapes=[pltpu.VMEM((2, tm, d), F32), pltpu.SemaphoreType.DMA(())]),
        compiler_params=_params(("arbitrary",), vm),
        name="moe_combine",
    )(slots[:, 0], slots[:, 1], h, g_next.reshape(1, d).astype(F32), dn)


def _odd_layer(h, hn, j, prm):
    z = matmul(hn, [prm["sg_in_proj"][j].astype(BF16)], _epi_gelu, BF16, tm=1024, tn=512, name="sg_in_proj")
    d = h.shape[1]
    vn = layernorm(z, prm["sg_ln_w"][j], prm["sg_ln_b"][j], BF16, width=d, col_block=1)
    gated = spatial_gate(z, vn, prm["sg_w_spatial"][j], prm["sg_b_spatial"][j])
    tm = _pick(h.shape[0], (1024, 512, 256, 128, 64, 32, 16, 8))
    tn = _pick(d, (512, 256, 128))
    return matmul(gated, [prm["sg_out_proj"][j].astype(BF16)], _epi_residual, F32, tm=tm, tn=tn,
                  extras=[_res_extra(h, tm, tn, 1)], name="sg_out_proj")


def kernel(x, p, positions, norm_mix, norm_ffn, norm_ple, final_norm, ev_in_proj, rw_mu, rw_w0, rw_w2, rw_a0, rw_a2, rw_g2, rw_k_k, rw_k_a, rw_r_k, rw_lnx_w, rw_lnx_b, mla_q_norm, mla_w_uq, mla_kv_norm, mla_w_ukv, ev_out_proj, ffn_w_gate, ffn_w_up, ffn_w_down, sg_in_proj, sg_ln_w, sg_ln_b, sg_w_spatial, sg_b_spatial, sg_out_proj, moe_router, moe_w_gate, moe_w_up, moe_w_down, ple_proj, ple_gate):
    prm = dict(ev_in_proj=ev_in_proj, rw_mu=rw_mu, rw_w0=rw_w0, rw_w2=rw_w2, rw_a0=rw_a0, rw_a2=rw_a2, rw_g2=rw_g2,
               rw_k_k=rw_k_k, rw_k_a=rw_k_a, rw_r_k=rw_r_k, rw_lnx_w=rw_lnx_w, rw_lnx_b=rw_lnx_b,
               mla_q_norm=mla_q_norm, mla_w_uq=mla_w_uq, mla_kv_norm=mla_kv_norm, mla_w_ukv=mla_w_ukv,
               ev_out_proj=ev_out_proj, sg_in_proj=sg_in_proj, sg_ln_w=sg_ln_w, sg_ln_b=sg_ln_b,
               sg_w_spatial=sg_w_spatial, sg_b_spatial=sg_b_spatial, sg_out_proj=sg_out_proj)
    bsz, s, d = x.shape
    assert bsz == 1
    inv = 1.0 / (ROPE_THETA ** (jnp.arange(0, QK_ROPE, 2, dtype=F32) / QK_ROPE))
    ang = positions[0].astype(F32)[:, None] * inv
    cos, sin = jnp.cos(ang), jnp.sin(ang)
    h = x[0]
    for i in range(DEPTH):
        j = i // 2
        if i % 2 == 0:
            hn = rmsnorm(h, norm_mix[i], BF16)
            h = _even_layer(h, hn, j, cos, sin, prm)
            hn = rmsnorm(h, norm_ffn[i], BF16)
            h = _dense_ffn(h, hn, ffn_w_gate[j], ffn_w_up[j], ffn_w_down[j])
        else:
            hn = rmsnorm(h, norm_mix[i], BF16)
            h = _odd_layer(h, hn, j, prm)
            h, hn = _moe(h, norm_ffn[i], moe_router[j], moe_w_gate[j], moe_w_up[j], moe_w_down[j], norm_ple[i])
        if i % 2 == 0:
            hn = rmsnorm(h, norm_ple[i], BF16)
        h = ple(hn, ple_gate[i].astype(BF16), p[i, 0].astype(BF16), ple_proj[i].astype(BF16), h)
    return rmsnorm(h, final_norm, F32)[None]
```
